```python
import math
import jax, jax.numpy as jnp
from jax import lax
import numpy as np

D_MODEL = 1024
BATCH = 8
SEQ = 4096
DEPTH = 2
DEC_BATCH = 128
DEC_SEQ = 1
PAST_LEN = 16384
PAGE_SIZE = 128

HEAD_DIM = 64
ROT_DIM = HEAD_DIM // 4
ROPE_THETA = 500000.0
DILATED_GROUPS = ((128, 1), (512, 4), (2048, 16))
N_GROUPS_A = len(DILATED_GROUPS)
N_HEADS_A = D_MODEL // HEAD_DIM
N_HEADS_B = D_MODEL // HEAD_DIM
N_KV_HEADS_B = 2
GQA_B = N_HEADS_B // N_KV_HEADS_B
WINDOW_B = 128
BLOCK_B = 128
D_FF = 2816
CONV_W = 3
N_MIXERS = 2
N_LAYERS_A = (DEPTH + 1) // 2
N_LAYERS_B = DEPTH // 2
ALPHA = (2.0 * DEPTH) ** 0.25
BETA = (8.0 * DEPTH) ** -0.25
LN_EPS = 1e-5
NEG_INF = -1e30
SCALE = 1.0 / math.sqrt(HEAD_DIM)

kernel_name = "hybrid_dilated_swa_sink_convffn_step"


def rope(x, pos):
    half = ROT_DIM // 2
    inv_freq = ROPE_THETA ** (-jnp.arange(half, dtype=jnp.float32) * 2.0 / ROT_DIM)
    ang = pos.astype(jnp.float32)[:, None] * inv_freq[None, :]
    ang = ang.reshape((ang.shape[0],) + (1,) * (x.ndim - 3) + (half,))
    cos, sin = jnp.cos(ang), jnp.sin(ang)
    x1, x2, rest = x[..., :half], x[..., half:ROT_DIM], x[..., ROT_DIM:]
    return jnp.concatenate([(x1 * cos - x2 * sin).astype(x.dtype),
                            (x2 * cos + x1 * sin).astype(x.dtype), rest], axis=-1)


def layer_norm(x, g, b):
    xf = x.astype(jnp.float32)
    mu = xf.mean(-1, keepdims=True)
    var = jnp.square(xf - mu).mean(-1, keepdims=True)
    return ((xf - mu) * lax.rsqrt(var + LN_EPS) * g + b).astype(x.dtype)


def band_pair(xb):
    prev = jnp.concatenate([jnp.zeros_like(xb[:, :1]), xb[:, :-1]], axis=1)
    return jnp.concatenate([prev, xb], axis=2)


def masked_softmax_lse(s, mask):
    s = jnp.where(mask, s, NEG_INF)
    m = s.max(axis=-1, keepdims=True)
    e = jnp.exp(s - m)
    den = e.sum(axis=-1, keepdims=True)
    return e / den, (m + jnp.log(den))[..., 0]


def sink_softmax(s, mask, sinks):
    sink = sinks.astype(jnp.float32).reshape((1, N_KV_HEADS_B, GQA_B) + (1,) * (s.ndim - 3))
    s = jnp.where(mask, s, NEG_INF)
    m = jnp.maximum(s.max(axis=-1, keepdims=True), sink)
    e = jnp.exp(s - m)
    return e / (e.sum(axis=-1, keepdims=True) + jnp.exp(sink - m))


def merge_by_denominator(outs, lses):
    o = jnp.stack(outs)
    w = jax.nn.softmax(jnp.stack(lses), axis=0)
    return jnp.sum(w[..., None] * o, axis=0)


def a_project(x, pos, w_qkv):
    B, T, _ = x.shape
    qkv = (x @ w_qkv).reshape(B, T, N_GROUPS_A, 3, N_HEADS_A, HEAD_DIM)
    return rope(qkv[:, :, :, 0], pos), rope(qkv[:, :, :, 1], pos), qkv[:, :, :, 2]


def dilated_band_attention(q, k, v, win, dil):
    B, S, H, Dh = q.shape
    band = win // dil
    L = S // dil
    nb = -(-L // band)
    Lp = nb * band

    def to_classes(t):
        t = t.reshape(B, L, dil, H, Dh).transpose(0, 2, 1, 3, 4).reshape(B * dil, L, H, Dh)
        return jnp.pad(t, ((0, 0), (0, Lp - L), (0, 0), (0, 0))).reshape(B * dil, nb, band, H, Dh)

    qb = to_classes(q)
    kk = band_pair(to_classes(k))
    vv = band_pair(to_classes(v))
    s = jnp.einsum('nbqhd,nbphd->nhbqp', qb, kk).astype(jnp.float32) * SCALE
    i = jnp.arange(band)[:, None]
    p = jnp.arange(2 * band)[None, :]
    dist = band + i - p
    blk = jnp.arange(nb)[:, None, None]
    mask = (dist >= 0) & (dist <= band) & ((blk > 0) | (p >= band))
    probs, lse = masked_softmax_lse(s, mask)
    o = jnp.einsum('nhbqp,nbphd->nbqhd', probs, vv)
    o = o.reshape(B, dil, Lp, H, Dh)[:, :, :L].transpose(0, 2, 1, 3, 4).reshape(B, S, H, Dh)
    lse = lse.transpose(0, 2, 3, 1).reshape(B, dil, Lp, H)[:, :, :L].transpose(0, 2, 1, 3).reshape(B, S, H)
    return o, lse


def mixer_a_prompt(x, pos, w_qkv, w_o):
    B, S, _ = x.shape
    q, k, v = a_project(x, pos, w_qkv)
    outs, lses, states = [], [], []
    for g, (win, dil) in enumerate(DILATED_GROUPS):
        o, lse = dilated_band_attention(q[:, :, g], k[:, :, g], v[:, :, g], win, dil)
        outs.append(o)
        lses.append(lse)
        keep = min(win, S)
        states.append(jnp.stack([k[:, S - keep:, g], v[:, S - keep:, g]], axis=1))
    y = merge_by_denominator(outs, lses).astype(x.dtype).reshape(B, S, N_HEADS_A * HEAD_DIM) @ w_o
    return y, states


def mixer_a_sample(x, pos, bufs, w_qkv, w_o):
    B, T, _ = x.shape
    q, k, v = a_project(x, pos, w_qkv)
    outs, lses, states = [], [], []
    for g, (win, dil) in enumerate(DILATED_GROUPS):
        buf = bufs[g]
        L = buf.shape[2]
        new_st = jnp.concatenate([buf[:, :, T:], jnp.stack([k[:, :, g], v[:, :, g]], axis=1)], axis=2)
        idx = L + jnp.arange(T)[:, None] - dil * jnp.arange(win // dil + 1)[None, :]
        valid = idx >= 0
        in_head = (idx < T)[None, None, :, :, None, None]
        g_head = buf[:, :, jnp.clip(idx, 0, T - 1)]
        g_tail = new_st[:, :, jnp.clip(idx - T, 0, L - 1)]
        kvg = jnp.where(in_head, g_head, g_tail)
        s = jnp.einsum('bthd,btkhd->bhtk', q[:, :, g], kvg[:, 0]).astype(jnp.float32) * SCALE
        probs, lse = masked_softmax_lse(s, valid[None, None])
        outs.append(jnp.einsum('bhtk,btkhd->bthd', probs, kvg[:, 1]))
        lses.append(lse.transpose(0, 2, 1))
        states.append(new_st)
    y = merge_by_denominator(outs, lses).astype(x.dtype).reshape(B, T, N_HEADS_A * HEAD_DIM) @ w_o
    return y, states


def b_project(x, pos, w_qkv, b_qkv):
    B, T, _ = x.shape
    qkv = x @ w_qkv + b_qkv
    nq = N_HEADS_B * HEAD_DIM
    nkv = N_KV_HEADS_B * HEAD_DIM
    q = rope(qkv[..., :nq].reshape(B, T, N_KV_HEADS_B, GQA_B, HEAD_DIM), pos)
    k = rope(qkv[..., nq:nq + nkv].reshape(B, T, N_KV_HEADS_B, HEAD_DIM), pos)
    v = qkv[..., nq + nkv:].reshape(B, T, N_KV_HEADS_B, HEAD_DIM)
    return q, k, v


def mixer_b_prompt(x, pos, w_qkv, b_qkv, sinks, w_o, b_o):
    B, S, _ = x.shape
    q, k, v = b_project(x, pos, w_qkv, b_qkv)
    nb = S // BLOCK_B
    qb = q.reshape(B, nb, BLOCK_B, N_KV_HEADS_B, GQA_B, HEAD_DIM)
    kk = band_pair(k.reshape(B, nb, BLOCK_B, N_KV_HEADS_B, HEAD_DIM))
    vv = band_pair(v.reshape(B, nb, BLOCK_B, N_KV_HEADS_B, HEAD_DIM))
    s = jnp.einsum('bnqhgd,bnphd->bhgnqp', qb, kk).astype(jnp.float32) * SCALE
    i = jnp.arange(BLOCK_B)[:, None]
    p = jnp.arange(2 * BLOCK_B)[None, :]
    dist = BLOCK_B + i - p
    blk = jnp.arange(nb)[:, None, None]
    mask = (dist >= 0) & (dist < WINDOW_B) & ((blk > 0) | (p >= BLOCK_B))
    probs = sink_softmax(s, mask, sinks)
    o = jnp.einsum('bhgnqp,bnphd->bnqhgd', probs, vv).astype(x.dtype).reshape(B, S, N_HEADS_B * HEAD_DIM)
    keep = min(WINDOW_B, S)
    return o @ w_o + b_o, jnp.stack([k[:, S - keep:], v[:, S - keep:]], axis=1)


def mixer_b_sample(x, pos, buf, w_qkv, b_qkv, sinks, w_o, b_o):
    B, T, _ = x.shape
    q, k, v = b_project(x, pos, w_qkv, b_qkv)
    L = buf.shape[2]
    kf = jnp.concatenate([buf[:, 0], k], axis=1)
    vf = jnp.concatenate([buf[:, 1], v], axis=1)
    s = jnp.einsum('bthgd,brhd->bhgtr', q, kf).astype(jnp.float32) * SCALE
    dist = L + jnp.arange(T)[:, None] - jnp.arange(L + T)[None, :]
    mask = (dist >= 0) & (dist < WINDOW_B)
    probs = sink_softmax(s, mask, sinks)
    o = jnp.einsum('bhgtr,brhd->bthgd', probs, vf).astype(x.dtype).reshape(B, T, N_HEADS_B * HEAD_DIM)
    return o @ w_o + b_o, jnp.stack([kf[:, T:], vf[:, T:]], axis=1)


def conv_ffn(x, prev, w_up, conv_w, conv_b, w_down):
    T = x.shape[1]
    gu = x @ w_up
    g, u = gu[..., :D_FF], gu[..., D_FF:]
    gfull = jnp.concatenate([prev.astype(g.dtype), g], axis=1)
    gc = conv_b + sum(conv_w[j] * gfull[:, j:j + T] for j in range(CONV_W))
    return (jax.nn.silu(gc) * u) @ w_down, gfull[:, T:]


def setup_inputs(seed: int = 0) -> dict:
    key = jax.random.key(seed)
    ks = jax.random.split(key, 24)
    f32 = jnp.float32

    def nrm(k, shape, scale):
        return jax.random.normal(k, shape, f32) * scale

    x_prompt = nrm(ks[0], (BATCH, SEQ, D_MODEL), 1.0)
    x_sample = nrm(ks[1], (DEC_BATCH, DEC_SEQ, D_MODEL), 1.0)
    cache_a1 = nrm(ks[2], (N_LAYERS_A, DEC_BATCH, 2, min(DILATED_GROUPS[0][0], PAST_LEN), N_HEADS_A, HEAD_DIM), 1.0)
    cache_a2 = nrm(ks[3], (N_LAYERS_A, DEC_BATCH, 2, min(DILATED_GROUPS[1][0], PAST_LEN), N_HEADS_A, HEAD_DIM), 1.0)
    cache_a3 = nrm(ks[4], (N_LAYERS_A, DEC_BATCH, 2, min(DILATED_GROUPS[2][0], PAST_LEN), N_HEADS_A, HEAD_DIM), 1.0)
    cache_b = nrm(ks[5], (N_LAYERS_B, DEC_BATCH, 2, min(WINDOW_B, PAST_LEN), N_KV_HEADS_B, HEAD_DIM), 1.0)
    state_ffn = nrm(ks[6], (DEPTH, DEC_BATCH, CONV_W - 1, D_FF), 1.0)
    wa = nrm(ks[7], (N_LAYERS_A, D_MODEL, N_GROUPS_A, 3, N_HEADS_A * HEAD_DIM), D_MODEL ** -0.5)
    w_qkv_a = (wa * jnp.array([1.0, 1.0, BETA], f32)[:, None]).reshape(N_LAYERS_A, D_MODEL, -1)
    w_o_a = nrm(ks[8], (N_LAYERS_A, N_HEADS_A * HEAD_DIM, D_MODEL), (N_HEADS_A * HEAD_DIM) ** -0.5 * BETA)
    col_scale_b = jnp.concatenate([jnp.ones(((N_HEADS_B + N_KV_HEADS_B) * HEAD_DIM,), f32),
                                   jnp.full((N_KV_HEADS_B * HEAD_DIM,), BETA, f32)])
    w_qkv_b = nrm(ks[9], (N_LAYERS_B, D_MODEL, (N_HEADS_B + 2 * N_KV_HEADS_B) * HEAD_DIM), D_MODEL ** -0.5) * col_scale_b
    b_qkv_b = nrm(ks[10], (N_LAYERS_B, (N_HEADS_B + 2 * N_KV_HEADS_B) * HEAD_DIM), 0.02)
    sinks_b = nrm(ks[11], (N_LAYERS_B, N_HEADS_B), 0.5)
    w_o_b = nrm(ks[12], (N_LAYERS_B, N_HEADS_B * HEAD_DIM, D_MODEL), (N_HEADS_B * HEAD_DIM) ** -0.5 * BETA)
    b_o_b = nrm(ks[13], (N_LAYERS_B, D_MODEL), 0.02)
    w_up = nrm(ks[14], (DEPTH, D_MODEL, 2 * D_FF), D_MODEL ** -0.5)
    conv_w = nrm(ks[15], (DEPTH, CONV_W, D_FF), 0.5)
    conv_b = nrm(ks[16], (DEPTH, D_FF), 0.02)
    w_down = nrm(ks[17], (DEPTH, D_FF, D_MODEL), D_FF ** -0.5 * BETA)
    ln_g = 1.0 + nrm(ks[18], (DEPTH, 2, D_MODEL), 0.02)
    ln_b = nrm(ks[19], (DEPTH, 2, D_MODEL), 0.02)
    return {"x_prompt": x_prompt, "x_sample": x_sample,
            "cache_a1": cache_a1, "cache_a2": cache_a2, "cache_a3": cache_a3,
            "cache_b": cache_b, "state_ffn": state_ffn,
            "w_qkv_a": w_qkv_a, "w_o_a": w_o_a,
            "w_qkv_b": w_qkv_b, "b_qkv_b": b_qkv_b, "sinks_b": sinks_b, "w_o_b": w_o_b, "b_o_b": b_o_b,
            "w_up": w_up, "conv_w": conv_w, "conv_b": conv_b, "w_down": w_down,
            "ln_g": ln_g, "ln_b": ln_b}


def reference(x_prompt, x_sample, cache_a1, cache_a2, cache_a3, cache_b, state_ffn,
              w_qkv_a, w_o_a, w_qkv_b, b_qkv_b, sinks_b, w_o_b, b_o_b,
              w_up, conv_w, conv_b, w_down, ln_g, ln_b):
    caches_a = (cache_a1, cache_a2, cache_a3)
    pos_p = jnp.arange(x_prompt.shape[1], dtype=jnp.int32)
    pos_s = PAST_LEN + jnp.arange(x_sample.shape[1], dtype=jnp.int32)
    xp, xs = x_prompt, x_sample
    new_a_p = [[] for _ in range(N_GROUPS_A)]
    new_a_s = [[] for _ in range(N_GROUPS_A)]
    new_b_p, new_b_s, new_f_p, new_f_s = [], [], [], []
    for i in range(DEPTH):
        j = i // N_MIXERS
        if i % N_MIXERS == 0:
            hp, st_p = mixer_a_prompt(xp, pos_p, w_qkv_a[j], w_o_a[j])
            hs, st_s = mixer_a_sample(xs, pos_s, [c[j] for c in caches_a], w_qkv_a[j], w_o_a[j])
            for g in range(N_GROUPS_A):
                new_a_p[g].append(st_p[g])
                new_a_s[g].append(st_s[g])
        else:
            hp, st_p = mixer_b_prompt(xp, pos_p, w_qkv_b[j], b_qkv_b[j], sinks_b[j], w_o_b[j], b_o_b[j])
            hs, st_s = mixer_b_sample(xs, pos_s, cache_b[j], w_qkv_b[j], b_qkv_b[j], sinks_b[j], w_o_b[j], b_o_b[j])
            new_b_p.append(st_p)
            new_b_s.append(st_s)
        xp = layer_norm(ALPHA * xp + hp, ln_g[i, 0], ln_b[i, 0])
        xs = layer_norm(ALPHA * xs + hs, ln_g[i, 0], ln_b[i, 0])
        zeros_prev = jnp.zeros((xp.shape[0], CONV_W - 1, D_FF), xp.dtype)
        fp, sp = conv_ffn(xp, zeros_prev, w_up[i], conv_w[i], conv_b[i], w_down[i])
        fs, ss = conv_ffn(xs, state_ffn[i], w_up[i], conv_w[i], conv_b[i], w_down[i])
        new_f_p.append(sp)
        new_f_s.append(ss)
        xp = layer_norm(ALPHA * xp + fp, ln_g[i, 1], ln_b[i, 1])
        xs = layer_norm(ALPHA * xs + fs, ln_g[i, 1], ln_b[i, 1])
    return (xp, xs,
            jnp.stack(new_a_p[0]), jnp.stack(new_a_s[0]),
            jnp.stack(new_a_p[1]), jnp.stack(new_a_s[1]),
            jnp.stack(new_a_p[2]), jnp.stack(new_a_s[2]),
            jnp.stack(new_b_p), jnp.stack(new_b_s),
            jnp.stack(new_f_p), jnp.stack(new_f_s))
```

```python
import functools
import math

import jax
import jax.numpy as jnp
from jax import lax
from jax.experimental import pallas as pl
from jax.experimental.pallas import tpu as pltpu

F32 = jnp.float32
BF16 = jnp.bfloat16

HEAD_DIM = 64
ROT_HALF = HEAD_DIM // 8
ROPE_THETA = 500000.0
DILATIONS = (1, 4, 16)
BAND = 128
WINDOW_B = 128
N_KV_HEADS_B = 2
PAST_LEN = 16384
CONV_W = 3
LN_EPS = 1e-5
NEG_INF = -1e30
SCALE = 1.0 / math.sqrt(HEAD_DIM)

LANES = 128
GROUP = 4 * HEAD_DIM
UNITS_PER_STEP = 16
VMEM_LIMIT = 56 * 1024 * 1024


def _cparams(sem):
    return pltpu.CompilerParams(dimension_semantics=sem, vmem_limit_bytes=VMEM_LIMIT)


def _resident(shape):
    nd = len(shape)
    return pl.BlockSpec(shape, lambda *_: (0,) * nd, pipeline_mode=pl.Buffered(1))


def _rope_lane_tables(pos, width):
    inv_freq = ROPE_THETA ** (-jnp.arange(ROT_HALF, dtype=F32) * 2.0 / (2 * ROT_HALF))
    ang = pos.astype(F32)[:, None] * inv_freq[None, :]
    cos, sin = jnp.cos(ang), jnp.sin(ang)
    n = pos.shape[0]
    one = jnp.ones((n, HEAD_DIM - 2 * ROT_HALF), F32)
    zero = jnp.zeros((n, HEAD_DIM - 2 * ROT_HALF), F32)
    zh = jnp.zeros((n, ROT_HALF), F32)
    c = jnp.concatenate([cos, cos, one], axis=1)
    sa = jnp.concatenate([-sin, zh, zero], axis=1)
    sb = jnp.concatenate([zh, sin, zero], axis=1)
    reps = width // HEAD_DIM
    return tuple(jnp.tile(t, (1, reps)) for t in (c, sa, sb))


def _rope_chunks(y, c, sa, sb):
    outs = []
    for j in range(y.shape[1] // LANES):
        yc = y[:, j * LANES:(j + 1) * LANES]
        outs.append(yc * c + pltpu.roll(yc, LANES - ROT_HALF, 1) * sa + pltpu.roll(yc, ROT_HALF, 1) * sb)
    return outs[0] if len(outs) == 1 else jnp.concatenate(outs, axis=1)


def _proj_kernel(*refs, wq, wkv, dil, tm, first_tile, st_rows, has_bias):
    if has_bias:
        x_ref, w_ref, b_ref, c_ref, sa_ref, sb_ref, q_ref, k_ref, v_ref, kst_ref, vst_ref, scr_ref = refs
    else:
        x_ref, w_ref, c_ref, sa_ref, sb_ref, q_ref, k_ref, v_ref, kst_ref, vst_ref, scr_ref = refs
        b_ref = None
    i = pl.program_id(1)
    xb = x_ref[0].astype(BF16)
    c, sa, sb = c_ref[...], sa_ref[...], sb_ref[...]

    def project(lo, width):
        y = jnp.dot(xb, w_ref[:, lo:lo + width], preferred_element_type=F32)
        if b_ref is not None:
            y = y + b_ref[:, lo:lo + width]
        return y

    def emit(y, out_ref):
        n = y.shape[1]
        if dil == 1:
            out_ref[0] = y.astype(BF16)
        else:
            for j in range(n // LANES):
                scr_ref[j] = y[:, j * LANES:(j + 1) * LANES]
            for r in range(dil):
                for j in range(n // LANES):
                    lo = r * n + j * LANES
                    out_ref[0, :, lo:lo + LANES] = scr_ref.at[j][pl.ds(r, tm // dil, stride=dil), :].astype(BF16)

    yq = _rope_chunks(project(0, wq), c, sa, sb) * SCALE
    emit(yq, q_ref)
    yk = _rope_chunks(project(wq, wkv), c, sa, sb)
    emit(yk, k_ref)
    yv = project(wq + wkv, wkv)
    emit(yv, v_ref)

    @pl.when(i >= first_tile)
    def _():
        kst_ref[0] = yk[tm - st_rows:, :]
        vst_ref[0] = yv[tm - st_rows:, :]


def _proj_prompt(x, w, bias, tabs, *, wq, wkv, dil, keep, tm):
    B, S, D = x.shape
    nt = S // tm
    st_rows = min(tm, keep)
    first_tile = (S - keep) // tm if keep >= tm else nt - 1
    has_bias = bias is not None
    L = S // dil

    def st_map(b, i):
        return (b, jnp.maximum(i - first_tile, 0), 0)

    in_specs = [pl.BlockSpec((1, tm, D), lambda b, i: (b, i, 0)), _resident((D, wq + 2 * wkv))]
    args = [x, w]
    if has_bias:
        in_specs.append(_resident((1, wq + 2 * wkv)))
        args.append(bias)
    for t in tabs:
        in_specs.append(pl.BlockSpec((tm, LANES), lambda b, i: (i, 0)))
        args.append(t)
    out_shape = [jax.ShapeDtypeStruct((B, L, dil * wq), BF16),
                 jax.ShapeDtypeStruct((B, L, dil * wkv), BF16),
                 jax.ShapeDtypeStruct((B, L, dil * wkv), BF16),
                 jax.ShapeDtypeStruct((B, keep, wkv), F32),
                 jax.ShapeDtypeStruct((B, keep, wkv), F32)]
    out_specs = [pl.BlockSpec((1, tm // dil, dil * wq), lambda b, i: (b, i, 0)),
                 pl.BlockSpec((1, tm // dil, dil * wkv), lambda b, i: (b, i, 0)),
                 pl.BlockSpec((1, tm // dil, dil * wkv), lambda b, i: (b, i, 0)),
                 pl.BlockSpec((1, st_rows, wkv), st_map),
                 pl.BlockSpec((1, st_rows, wkv), st_map)]
    kern = functools.partial(_proj_kernel, wq=wq, wkv=wkv, dil=dil, tm=tm, first_tile=first_tile,
                             st_rows=st_rows, has_bias=has_bias)
    return pl.pallas_call(
        kern, grid=(B, nt), in_specs=in_specs, out_specs=out_specs, out_shape=out_shape,
        scratch_shapes=[pltpu.VMEM((max(wq, wkv) // LANES, tm, LANES), F32)],
        compiler_params=_cparams(("arbitrary", "arbitrary")),
        name=f"proj_prompt_d{dil}",
    )(*args)


def _head_stack(x, axis):
    head = lax.broadcasted_iota(jnp.int32, x.shape, 1) // HEAD_DIM
    zero = jnp.zeros_like(x)
    return jnp.concatenate([jnp.where(head == h, x, zero) for h in range(GROUP // HEAD_DIM)], axis=axis)


def _band_unit(q, kk, vst, mask, sinks):
    nh = GROUP // HEAD_DIM
    s = lax.dot_general(_head_stack(q, 0), kk, (((1,), (1,)), ((), ())), preferred_element_type=F32)
    ps, lses = [], []
    for h in range(nh):
        sh = jnp.where(mask, s[h * BAND:(h + 1) * BAND], NEG_INF)
        m = jnp.max(sh, axis=-1, keepdims=True)
        if sinks is not None:
            m = jnp.maximum(m, sinks[h])
        e = jnp.exp(sh - m)
        den = jnp.sum(e, axis=-1, keepdims=True)
        if sinks is not None:
            den = den + jnp.exp(sinks[h] - m)
        ps.append((e * (1.0 / den)).astype(BF16))
        lses.append(m + jnp.log(den))
    p = jnp.concatenate([ps[h][:, :BAND] for h in range(nh)] + [ps[h][:, BAND:] for h in range(nh)], axis=1)
    return jnp.dot(p, vst, preferred_element_type=F32), lses


def _band_attn_kernel(*refs, R, CG, gpc, gqa, max_dist_inclusive, want_lse):
    if gqa:
        q_ref, k_ref, kp_ref, v_ref, vp_ref, sink_ref, o_ref = refs
        lse_ref = None
    else:
        q_ref, k_ref, kp_ref, v_ref, vp_ref, o_ref, lse_ref = refs
        sink_ref = None
        lse_ref[...] = jnp.zeros(lse_ref.shape, F32)
    i = pl.program_id(2)
    row = lax.broadcasted_iota(jnp.int32, (BAND, 2 * BAND), 0)
    col = lax.broadcasted_iota(jnp.int32, (BAND, 2 * BAND), 1)
    dist = BAND + row - col
    hi = BAND if max_dist_inclusive else BAND - 1
    band = (dist >= 0) & (dist <= hi)
    cur = col >= BAND

    def kv_block(ref, pref, j, c):
        src, lo = (pref, 0) if j < 0 else (ref, j * BAND)
        if not gqa:
            return src[0, lo:lo + BAND, c * GROUP:(c + 1) * GROUP]
        g = c // (CG // N_KV_HEADS_B)
        x = src[0, lo:lo + BAND, g * HEAD_DIM:(g + 1) * HEAD_DIM]
        return jnp.concatenate([x] * (GROUP // HEAD_DIM), axis=1)

    for c in range(CG):
        sinks = None
        if gqa:
            sinks = [sink_ref[0:1, c * 4 + h:c * 4 + h + 1] for h in range(4)]
        k_prev = kv_block(k_ref, kp_ref, -1, c)
        vs_prev = _head_stack(kv_block(v_ref, vp_ref, -1, c), 0)
        for j in range(R):
            k_cur = kv_block(k_ref, kp_ref, j, c)
            vs_cur = _head_stack(kv_block(v_ref, vp_ref, j, c), 0)
            if j == 0:
                mask = band & (cur | (i > 0))
            else:
                mask = band
            q = q_ref[0, j * BAND:(j + 1) * BAND, c * GROUP:(c + 1) * GROUP]
            o, lses = _band_unit(q, jnp.concatenate([k_prev, k_cur], axis=0),
                                 jnp.concatenate([vs_prev, vs_cur], axis=0), mask, sinks)
            o_ref[0, j * BAND:(j + 1) * BAND, c * GROUP:(c + 1) * GROUP] = o.astype(o_ref.dtype)
            if want_lse:
                for h in range(4):
                    lane = (c % gpc) * 4 + h
                    lse_ref[0, c // gpc, j * BAND:(j + 1) * BAND, lane:lane + 1] = lses[h]
            k_prev, vs_prev = k_cur, vs_cur


def _band_attn(q, k, v, sinks, *, R, CG, gpc, gqa):
    B, L, C = q.shape
    ncol = C // (CG * GROUP)
    nrow = L // (R * BAND)
    kw = k.shape[2] if gqa else CG * GROUP

    def cur_map(b, ci, i):
        return (b, i, ci)

    def kcur_map(b, ci, i):
        return (b, i, 0 if gqa else ci)

    def kprev_map(b, ci, i):
        return (b, jnp.maximum(i * R - 1, 0), 0 if gqa else ci)

    in_specs = [pl.BlockSpec((1, R * BAND, CG * GROUP), cur_map),
                pl.BlockSpec((1, R * BAND, kw), kcur_map),
                pl.BlockSpec((1, BAND, kw), kprev_map),
                pl.BlockSpec((1, R * BAND, kw), kcur_map),
                pl.BlockSpec((1, BAND, kw), kprev_map)]
    args = [q, k, k, v, v]
    out_shape = [jax.ShapeDtypeStruct((B, L, C), BF16)]
    out_specs = [pl.BlockSpec((1, R * BAND, CG * GROUP), cur_map)]
    if gqa:
        in_specs.append(_resident((1, C // HEAD_DIM)))
        args.append(sinks)
    else:
        out_shape.append(jax.ShapeDtypeStruct((B, C // (gpc * GROUP), L, LANES), F32))
        out_specs.append(pl.BlockSpec((1, CG // gpc, R * BAND, LANES), lambda b, ci, i: (b, ci, i, 0)))
    kern = functools.partial(_band_attn_kernel, R=R, CG=CG, gpc=gpc, gqa=gqa, max_dist_inclusive=not gqa,
                             want_lse=not gqa)
    return pl.pallas_call(
        kern, grid=(B, ncol, nrow), in_specs=in_specs, out_specs=out_specs, out_shape=out_shape,
        compiler_params=_cparams(("arbitrary", "arbitrary", "arbitrary")),
        name="band_attn_gqa" if gqa else f"band_attn_c{C}",
    )(*args)


def _layer_norm(z, g, b):
    mu = jnp.mean(z, axis=-1, keepdims=True)
    zc = z - mu
    var = jnp.mean(zc * zc, axis=-1, keepdims=True)
    return zc * lax.rsqrt(var + LN_EPS) * g + b


def _oproj_kernel(*refs, dils, tm, alpha, has_bias):
    nb = len(dils)
    o_refs = refs[:nb]
    pos = nb
    lse_refs = ()
    if nb > 1:
        lse_refs = refs[pos:pos + nb]
        pos += nb
        e_ref = refs[pos]
        pos += 1
    x_ref, w_ref = refs[pos], refs[pos + 1]
    pos += 2
    b_ref = None
    if has_bias:
        b_ref = refs[pos]
        pos += 1
    g_ref, beta_ref, out_ref = refs[pos], refs[pos + 1], refs[pos + 2]
    scr = refs[pos + 3:]

    if nb == 1:
        merged = o_refs[0][0].astype(BF16)
    else:
        os_, ls = [], []
        si = 0
        for n, dil in enumerate(dils):
            if dil == 1:
                os_.append(o_refs[n][0].astype(F32))
                ls.append(lse_refs[n][0, 0])
            else:
                so, sl = scr[si], scr[si + 1]
                si += 2
                nch = so.shape[0]
                for r in range(dil):
                    for jc in range(nch):
                        lo = (r * nch + jc) * LANES
                        so.at[jc][pl.ds(r, tm // dil, stride=dil), :] = o_refs[n][0, :, lo:lo + LANES].astype(F32)
                    sl[pl.ds(r, tm // dil, stride=dil), :] = lse_refs[n][0, r]
                os_.append(jnp.concatenate([so[jc] for jc in range(nch)], axis=1))
                ls.append(sl[...])
        m = functools.reduce(jnp.maximum, ls)
        ws = [jnp.exp(l - m) for l in ls]
        tot = functools.reduce(lambda a, b: a + b, ws)
        acc = None
        for o, w_ in zip(os_, ws):
            w_ = w_ / tot
            w_hi = w_.astype(BF16)
            w_lo = (w_ - w_hi.astype(F32)).astype(BF16)
            wexp = (jnp.dot(w_hi, e_ref[...], preferred_element_type=F32)
                    + jnp.dot(w_lo, e_ref[...], preferred_element_type=F32))
            acc = wexp * o if acc is None else acc + wexp * o
        merged = acc.astype(BF16)
    y = jnp.dot(merged, w_ref[...], preferred_element_type=F32)
    if b_ref is not None:
        y = y + b_ref[...]
    z = alpha * x_ref[0] + y
    out_ref[0] = _layer_norm(z, g_ref[...], beta_ref[...])


def _oproj_ln(os_, lses, x, w, bias, g, beta, *, dils, tm, alpha):
    B, S, D = x.shape
    C = w.shape[0]
    nb = len(dils)
    in_specs, args, scratch = [], [], []
    for o, dil in zip(os_, dils):
        in_specs.append(pl.BlockSpec((1, tm // dil, dil * C), lambda b, i: (b, i, 0)))
        args.append(o)
    if nb > 1:
        for l, dil in zip(lses, dils):
            in_specs.append(pl.BlockSpec((1, dil, tm // dil, LANES), lambda b, i: (b, 0, i, 0)))
            args.append(l)
            if dil > 1:
                scratch += [pltpu.VMEM((C // LANES, tm, LANES), F32), pltpu.VMEM((tm, LANES), F32)]
        expand = (jnp.arange(C)[None, :] // HEAD_DIM == jnp.arange(LANES)[:, None]).astype(BF16)
        in_specs.append(_resident((LANES, C)))
        args.append(expand)
    in_specs += [pl.BlockSpec((1, tm, D), lambda b, i: (b, i, 0)), _resident((C, D))]
    args += [x, w]
    if bias is not None:
        in_specs.append(_resident((1, D)))
        args.append(bias)
    in_specs += [_resident((1, D))] * 2
    args += [g, beta]
    kern = functools.partial(_oproj_kernel, dils=dils, tm=tm, alpha=alpha, has_bias=bias is not None)
    return pl.pallas_call(
        kern, grid=(B, S // tm), in_specs=in_specs,
        out_specs=pl.BlockSpec((1, tm, D), lambda b, i: (b, i, 0)),
        out_shape=jax.ShapeDtypeStruct((B, S, D), F32),
        scratch_shapes=scratch,
        compiler_params=_cparams(("arbitrary", "arbitrary")),
        name=f"oproj_ln_{nb}",
    )(*args)


FF_CHUNK = 256


def _ffn_kernel(*refs, tm, ff, alpha, seq_conv):
    if seq_conv:
        x_ref, wup_ref, cw_ref, cb_ref, wdn_ref, g_ref, beta_ref, out_ref, st_ref, g_scr, carry_scr, act_scr = refs
    else:
        (x_ref, p0_ref, p1_ref, wup_ref, cw_ref, cb_ref, wdn_ref, g_ref, beta_ref,
         out_ref, st_ref, act_scr) = refs
    i = pl.program_id(1)
    last = pl.num_programs(1) - 1
    x = x_ref[0]
    xb = x.astype(BF16)
    for c in range(ff // FF_CHUNK):
        lo = c * FF_CHUNK
        gate = jnp.dot(xb, wup_ref[:, lo:lo + FF_CHUNK], preferred_element_type=F32)
        up = jnp.dot(xb, wup_ref[:, ff + lo:ff + lo + FF_CHUNK], preferred_element_type=F32)
        if seq_conv:
            carry = carry_scr[:, lo:lo + FF_CHUNK]
            g_scr[0:8, :] = jnp.where(i > 0, carry, jnp.zeros_like(carry))
            g_scr[8:8 + tm, :] = gate
            g1 = g_scr[7:7 + tm, :]
            g2 = g_scr[6:6 + tm, :]
            carry_scr[:, lo:lo + FF_CHUNK] = gate[tm - 8:, :]

            @pl.when(i == last)
            def _():
                st_ref[0, :, lo:lo + FF_CHUNK] = gate[tm - (CONV_W - 1):, :]
        else:
            g2 = p0_ref[:, lo:lo + FF_CHUNK]
            g1 = p1_ref[:, lo:lo + FF_CHUNK]
            st_ref[0, :, lo:lo + FF_CHUNK] = gate
        cw = cw_ref[:, lo:lo + FF_CHUNK]
        gc = cb_ref[:, lo:lo + FF_CHUNK] + cw[0:1] * g2 + cw[1:2] * g1 + cw[2:3] * gate
        act = gc * jax.nn.sigmoid(gc) * up
        act_scr[:, lo:lo + FF_CHUNK] = act.astype(BF16)
    y = jnp.dot(act_scr[...], wdn_ref[...], preferred_element_type=F32)
    out_ref[0] = _layer_norm(alpha * x + y, g_ref[...], beta_ref[...])


def _ffn_ln(x, prev, w_up, cw, cb, w_dn, g, beta, *, tm, alpha):
    B, S, D = x.shape
    ff = w_dn.shape[0]
    seq_conv = prev is None
    in_specs = [pl.BlockSpec((1, tm, D), lambda b, i: (b, i, 0))]
    args = [x]
    if not seq_conv:
        in_specs += [pl.BlockSpec((tm, ff), lambda b, i: (i, 0))] * 2
        args += list(prev)
    in_specs += [_resident((D, 2 * ff)), _resident((CONV_W, ff)), _resident((1, ff)), _resident((ff, D)),
                 _resident((1, D)), _resident((1, D))]
    args += [w_up, cw, cb, w_dn, g, beta]
    if seq_conv:
        st_shape = jax.ShapeDtypeStruct((B, CONV_W - 1, ff), F32)
        st_spec = pl.BlockSpec((1, CONV_W - 1, ff), lambda b, i: (b, 0, 0))
        scratch = [pltpu.VMEM((tm + 8, FF_CHUNK), F32), pltpu.VMEM((8, ff), F32), pltpu.VMEM((tm, ff), BF16)]
    else:
        st_shape = jax.ShapeDtypeStruct((B, S, ff), F32)
        st_spec = pl.BlockSpec((1, tm, ff), lambda b, i: (b, i, 0))
        scratch = [pltpu.VMEM((tm, ff), BF16)]
    kern = functools.partial(_ffn_kernel, tm=tm, ff=ff, alpha=alpha, seq_conv=seq_conv)
    return pl.pallas_call(
        kern, grid=(B, S // tm), in_specs=in_specs,
        out_specs=[pl.BlockSpec((1, tm, D), lambda b, i: (b, i, 0)), st_spec],
        out_shape=[jax.ShapeDtypeStruct((B, S, D), F32), st_shape],
        scratch_shapes=scratch,
        compiler_params=_cparams(("arbitrary", "arbitrary")),
        name="ffn_ln_seq" if seq_conv else "ffn_ln_rows",
    )(*args)


def _proj_sample_kernel(*refs, has_bias):
    if has_bias:
        x_ref, w_ref, b_ref, c_ref, sa_ref, sb_ref, nat_ref, tr_ref = refs
    else:
        x_ref, w_ref, c_ref, sa_ref, sb_ref, nat_ref, tr_ref = refs
    y = jnp.dot(x_ref[...].astype(BF16), w_ref[...], preferred_element_type=F32)
    if has_bias:
        y = y + b_ref[...]
    outs = []
    for j in range(y.shape[1] // LANES):
        sl = slice(j * LANES, (j + 1) * LANES)
        yc = y[:, sl]
        outs.append(yc * c_ref[:, sl] + pltpu.roll(yc, LANES - ROT_HALF, 1) * sa_ref[:, sl]
                    + pltpu.roll(yc, ROT_HALF, 1) * sb_ref[:, sl])
    y = jnp.concatenate(outs, axis=1)
    nat_ref[0] = y
    tr_ref[0] = y.T


def _proj_sample(x, w, bias, tabs, gw):
    N, D = x.shape
    n = w.shape[1] // gw
    in_specs = [pl.BlockSpec((N, D), lambda j: (0, 0)), pl.BlockSpec((D, gw), lambda j: (0, j))]
    args = [x, w]
    if bias is not None:
        in_specs.append(pl.BlockSpec((1, gw), lambda j: (0, j)))
        args.append(bias)
    in_specs += [pl.BlockSpec((1, gw), lambda j: (0, j))] * 3
    args += list(tabs)
    return pl.pallas_call(
        functools.partial(_proj_sample_kernel, has_bias=bias is not None),
        grid=(n,), in_specs=in_specs,
        out_specs=[pl.BlockSpec((1, N, gw), lambda j: (j, 0, 0)), pl.BlockSpec((1, gw, N), lambda j: (j, 0, 0))],
        out_shape=[jax.ShapeDtypeStruct((n, N, gw), F32), jax.ShapeDtypeStruct((n, gw, N), F32)],
        compiler_params=_cparams(("arbitrary",)),
        name="proj_sample",
    )(*args)


def _pick_column(x, b):
    lane = lax.broadcasted_iota(jnp.int32, x.shape, 1)
    return jnp.sum(jnp.where(lane == b, x, 0.0), axis=1, keepdims=True)


def _roll_in(buf_t, new_col):
    L = buf_t.shape[1]
    lane = lax.broadcasted_iota(jnp.int32, buf_t.shape, 1)
    return jnp.where(lane == L - 1, new_col, pltpu.roll(buf_t, L - 1, 1))


def _sample_attn_a_kernel(qkv_ref, c1_ref, c2_ref, c3_ref, n1_ref, n2_ref, n3_ref, ot_ref, *, hpc):
    b = pl.program_id(0)
    hc = pl.program_id(1)
    rows = hpc * HEAD_DIM
    cols = [_pick_column(qkv_ref[t], b) for t in range(3 * len(DILATIONS))]
    nb = ot_ref.shape[1]
    lane_b = lax.broadcasted_iota(jnp.int32, (HEAD_DIM, nb), 1)

    @pl.when(b == 0)
    def _():
        ot_ref[pl.ds(pl.multiple_of(hc * rows, rows), rows), :] = jnp.zeros((rows, nb), F32)

    for hh in range(hpc):
        hs = slice(hh * HEAD_DIM, (hh + 1) * HEAD_DIM)
        outs, lses = [], []
        for g, (dil, c_ref, n_ref) in enumerate(zip(DILATIONS, (c1_ref, c2_ref, c3_ref), (n1_ref, n2_ref, n3_ref))):
            qc, kc, vc = cols[3 * g][hs], cols[3 * g + 1][hs], cols[3 * g + 2][hs]
            kt = c_ref[0, 0, hh]
            vt = c_ref[0, 1, hh]
            L = kt.shape[1]
            lane = lax.broadcasted_iota(jnp.int32, (1, L), 1)
            s = jnp.sum(qc * kt, axis=0, keepdims=True)
            s = jnp.where(lane % dil == 0, s, NEG_INF)
            s_new = jnp.sum(qc * kc, axis=0, keepdims=True)
            m = jnp.maximum(jnp.max(s, axis=1, keepdims=True), s_new)
            e = jnp.exp(s - m)
            e_new = jnp.exp(s_new - m)
            den = jnp.sum(e, axis=1, keepdims=True) + e_new
            outs.append(jnp.sum((e / den) * vt, axis=1, keepdims=True) + (e_new / den) * vc)
            lses.append(m + jnp.log(den))
            n_ref[0, 0, hh] = _roll_in(kt, kc)
            n_ref[0, 1, hh] = _roll_in(vt, vc)
        m = functools.reduce(jnp.maximum, lses)
        ws = [jnp.exp(l - m) for l in lses]
        tot = functools.reduce(lambda a, c: a + c, ws)
        o = functools.reduce(lambda a, c: a + c, [(w_ / tot) * o_ for w_, o_ in zip(ws, outs)])
        r0 = pl.multiple_of(hc * rows + hh * HEAD_DIM, HEAD_DIM)
        old = ot_ref[pl.ds(r0, HEAD_DIM), :]
        ot_ref[pl.ds(r0, HEAD_DIM), :] = jnp.where(lane_b == b, o, old)


def _sample_attn_a(qkv_t, caches, *, hpc):
    _, C, N = qkv_t.shape
    H = C // HEAD_DIM

    def cspec(c):
        L = c.shape[-1]
        return pl.BlockSpec((1, 2, hpc, HEAD_DIM, L), lambda b, hc: (b, 0, hc, 0, 0))

    in_specs = [pl.BlockSpec((qkv_t.shape[0], hpc * HEAD_DIM, N), lambda b, hc: (0, hc, 0))]
    in_specs += [cspec(c) for c in caches]
    out_specs = [cspec(c) for c in caches] + [pl.BlockSpec((C, N), lambda b, hc: (0, 0))]
    out_shape = [jax.ShapeDtypeStruct(c.shape, F32) for c in caches] + [jax.ShapeDtypeStruct((C, N), F32)]
    return pl.pallas_call(
        functools.partial(_sample_attn_a_kernel, hpc=hpc),
        grid=(N, H // hpc), in_specs=in_specs, out_specs=out_specs, out_shape=out_shape,
        compiler_params=_cparams(("arbitrary", "arbitrary")),
        name="sample_attn_a",
    )(qkv_t, *caches)


def _sample_attn_b_kernel(q_ref, k_ref, v_ref, sink_ref, c_ref, n_ref, ot_ref, *, n_heads):
    b = pl.program_id(0)
    nb = ot_ref.shape[1]
    gqa = n_heads // N_KV_HEADS_B
    qcol = _pick_column(q_ref[...], b)
    kcol = _pick_column(k_ref[...], b)
    vcol = _pick_column(v_ref[...], b)
    lane_b = lax.broadcasted_iota(jnp.int32, (HEAD_DIM, nb), 1)

    @pl.when(b == 0)
    def _():
        ot_ref[...] = jnp.zeros(ot_ref.shape, F32)

    for kv in range(N_KV_HEADS_B):
        ks = slice(kv * HEAD_DIM, (kv + 1) * HEAD_DIM)
        kc, vc = kcol[ks], vcol[ks]
        kt = c_ref[0, 0, kv]
        vt = c_ref[0, 1, kv]
        L = kt.shape[1]
        lane = lax.broadcasted_iota(jnp.int32, (1, L), 1)
        for hq in range(gqa):
            h = kv * gqa + hq
            qc = qcol[h * HEAD_DIM:(h + 1) * HEAD_DIM]
            sink = sink_ref[0:1, h:h + 1]
            s = jnp.sum(qc * kt, axis=0, keepdims=True)
            s = jnp.where(L - lane < WINDOW_B, s, NEG_INF)
            s_new = jnp.sum(qc * kc, axis=0, keepdims=True)
            m = jnp.maximum(jnp.maximum(jnp.max(s, axis=1, keepdims=True), s_new), sink)
            e = jnp.exp(s - m)
            e_new = jnp.exp(s_new - m)
            den = jnp.sum(e, axis=1, keepdims=True) + e_new + jnp.exp(sink - m)
            o = jnp.sum((e / den) * vt, axis=1, keepdims=True) + (e_new / den) * vc
            old = ot_ref[h * HEAD_DIM:(h + 1) * HEAD_DIM, :]
            ot_ref[h * HEAD_DIM:(h + 1) * HEAD_DIM, :] = jnp.where(lane_b == b, o, old)
        n_ref[0, 0, kv] = _roll_in(kt, kc)
        n_ref[0, 1, kv] = _roll_in(vt, vc)


def _sample_attn_b(q_t, k_t, v_t, sinks, cache):
    C, N = q_t.shape
    const = lambda b: (0, 0)
    cspec = pl.BlockSpec((1,) + cache.shape[1:], lambda b: (b, 0, 0, 0, 0))
    return pl.pallas_call(
        functools.partial(_sample_attn_b_kernel, n_heads=C // HEAD_DIM),
        grid=(N,),
        in_specs=[pl.BlockSpec(q_t.shape, const), pl.BlockSpec(k_t.shape, const), pl.BlockSpec(v_t.shape, const),
                  pl.BlockSpec(sinks.shape, const), cspec],
        out_specs=[cspec, pl.BlockSpec((C, N), const)],
        out_shape=[jax.ShapeDtypeStruct(cache.shape, F32), jax.ShapeDtypeStruct((C, N), F32)],
        compiler_params=_cparams(("arbitrary",)),
        name="sample_attn_b",
    )(q_t, k_t, v_t, sinks, cache)


def _to_positions_minor(c):
    return jnp.transpose(c, (0, 1, 3, 4, 2))


def _from_positions_minor(c):
    return jnp.transpose(c, (0, 1, 4, 2, 3))


def kernel(x_prompt, x_sample, cache_a1, cache_a2, cache_a3, cache_b, state_ffn, w_qkv_a, w_o_a, w_qkv_b,
           b_qkv_b, sinks_b, w_o_b, b_o_b, w_up, conv_w, conv_b, w_down, ln_g, ln_b):
    B, S, D = x_prompt.shape
    N, T, _ = x_sample.shape
    assert T == 1, "the sample group is decoded one token per request"
    depth = w_up.shape[0]
    ff = w_down.shape[1]
    alpha = (2.0 * depth) ** 0.25
    H = D // HEAD_DIM
    gpc = D // GROUP
    nq_b = w_o_b.shape[1]
    kvw_b = N_KV_HEADS_B * HEAD_DIM
    caches_a = (cache_a1, cache_a2, cache_a3)
    tm = min(512, S)

    pos_p = jnp.arange(S, dtype=jnp.int32)
    tabs_p = _rope_lane_tables(pos_p, LANES)
    pos_s = PAST_LEN + jnp.arange(T, dtype=jnp.int32)
    c_s, sa_s, sb_s = _rope_lane_tables(pos_s, D)

    xp = x_prompt
    xs = x_sample.reshape(1, N, D)
    new_a_p = [[] for _ in DILATIONS]
    new_a_s = [[] for _ in DILATIONS]
    new_b_p, new_b_s, new_f_p, new_f_s = [], [], [], []

    for i in range(depth):
        j = i // 2
        g1, b1 = ln_g[i, 0][None], ln_b[i, 0][None]
        g2, b2 = ln_g[i, 1][None], ln_b[i, 1][None]
        if i % 2 == 0:
            w_all = w_qkv_a[j].astype(BF16)
            wo = w_o_a[j].astype(BF16)
            os_, lses = [], []
            for g, dil in enumerate(DILATIONS):
                q, k, v, kst, vst = _proj_prompt(
                    xp, w_all[:, g * 3 * D:(g + 1) * 3 * D], None, tabs_p,
                    wq=D, wkv=D, dil=dil, keep=min(BAND * dil, S), tm=tm)
                R = min(4, (S // dil) // BAND)
                cg = gpc * max(1, min(dil, (UNITS_PER_STEP // R) // gpc))
                o, lse = _band_attn(q, k, v, None, R=R, CG=cg, gpc=gpc, gqa=False)
                os_.append(o)
                lses.append(lse)
                keep = kst.shape[1]
                new_a_p[g].append(jnp.stack([kst, vst], axis=1).reshape(B, 2, keep, H, HEAD_DIM))
            hp = _oproj_ln(os_, lses, xp, wo, None, g1, b1, dils=DILATIONS, tm=tm, alpha=alpha)
            one = jnp.ones((1, D), F32)
            zero = jnp.zeros((1, D), F32)
            c_tab = jnp.concatenate([c_s * SCALE, c_s, one] * len(DILATIONS), axis=1)
            sa_tab = jnp.concatenate([sa_s * SCALE, sa_s, zero] * len(DILATIONS), axis=1)
            sb_tab = jnp.concatenate([sb_s * SCALE, sb_s, zero] * len(DILATIONS), axis=1)
            nat, tr = _proj_sample(xs[0], w_all, None, (c_tab, sa_tab, sb_tab), D)
            outs = _sample_attn_a(tr, [_to_positions_minor(c[j]) for c in caches_a], hpc=min(4, H))
            for g in range(len(DILATIONS)):
                new_a_s[g].append(_from_positions_minor(outs[g]))
            hs = _oproj_ln([outs[-1].T[None]], None, xs, wo, None, g1, b1, dils=(1,), tm=N, alpha=alpha)
        else:
            w_all = w_qkv_b[j].astype(BF16)
            wo = w_o_b[j].astype(BF16)
            bias = b_qkv_b[j][None]
            sinks = sinks_b[j][None]
            q, k, v, kst, vst = _proj_prompt(xp, w_all, bias, tabs_p, wq=nq_b, wkv=kvw_b, dil=1,
                                             keep=min(WINDOW_B, S), tm=tm)
            (o,) = _band_attn(q, k, v, sinks, R=min(4, S // BAND), CG=nq_b // GROUP, gpc=nq_b // GROUP, gqa=True)
            keep = kst.shape[1]
            new_b_p.append(jnp.stack([kst, vst], axis=1).reshape(B, 2, keep, N_KV_HEADS_B, HEAD_DIM))
            hp = _oproj_ln([o], None, xp, wo, b_o_b[j][None], g1, b1, dils=(1,), tm=tm, alpha=alpha)
            nw = w_all.shape[1]
            reps = nw // D + 1
            c_full = jnp.concatenate([c_s * SCALE] + [c_s] * reps, axis=1)
            sa_full = jnp.concatenate([sa_s * SCALE] + [sa_s] * reps, axis=1)
            sb_full = jnp.concatenate([sb_s * SCALE] + [sb_s] * reps, axis=1)
            is_v = (jnp.arange(nw) >= nq_b + kvw_b)[None]
            c_tab = jnp.where(is_v, 1.0, c_full[:, :nw])
            sa_tab = jnp.where(is_v, 0.0, sa_full[:, :nw])
            sb_tab = jnp.where(is_v, 0.0, sb_full[:, :nw])
            nat, tr = _proj_sample(xs[0], w_all, bias, (c_tab, sa_tab, sb_tab), nw)
            tr = tr[0]
            new_c, o_t = _sample_attn_b(tr[:nq_b], tr[nq_b:nq_b + kvw_b], tr[nq_b + kvw_b:], sinks,
                                        _to_positions_minor(cache_b[j]))
            new_b_s.append(_from_positions_minor(new_c))
            hs = _oproj_ln([o_t.T[None]], None, xs, wo, b_o_b[j][None], g1, b1, dils=(1,), tm=N, alpha=alpha)
        wup = w_up[i].astype(BF16)
        wdn = w_down[i].astype(BF16)
        cb = conv_b[i][None]
        xp, st_p = _ffn_ln(hp, None, wup, conv_w[i], cb, wdn, g2, b2, tm=tm, alpha=alpha)
        xs, g_s = _ffn_ln(hs, (state_ffn[i][:, 0], state_ffn[i][:, 1]), wup, conv_w[i], cb, wdn, g2, b2,
                          tm=N, alpha=alpha)
        new_f_p.append(st_p)
        new_f_s.append(jnp.stack([state_ffn[i][:, 1], g_s[0]], axis=1))

    return (xp, xs.reshape(N, T, D),
            jnp.stack(new_a_p[0]), jnp.stack(new_a_s[0]),
            jnp.stack(new_a_p[1]), jnp.stack(new_a_s[1]),
            jnp.stack(new_a_p[2]), jnp.stack(new_a_s[2]),
            jnp.stack(new_b_p), jnp.stack(new_b_s),
            jnp.stack(new_f_p), jnp.stack(new_f_s))
```

```python
import functools
import math

import jax
import jax.numpy as jnp
from jax import lax
from jax.experimental import pallas as pl
from jax.experimental.pallas import tpu as pltpu

F32 = jnp.float32
BF16 = jnp.bfloat16

HEAD_DIM = 64
ROT_HALF = HEAD_DIM // 8
ROPE_THETA = 500000.0
DILATIONS = (1, 4, 16)
BAND = 128
WINDOW_B = 128
N_KV_HEADS_B = 2
PAST_LEN = 16384
CONV_W = 3
LN_EPS = 1e-5
NEG_INF = -1e30
SCALE = 1.0 / math.sqrt(HEAD_DIM)

LANES = 128
GROUP = 4 * HEAD_DIM
UNITS_PER_STEP = 16
VMEM_LIMIT = 56 * 1024 * 1024


def _cparams(sem):
    return pltpu.CompilerParams(dimension_semantics=sem, vmem_limit_bytes=VMEM_LIMIT)


def _resident(shape):
    nd = len(shape)
    return pl.BlockSpec(shape, lambda *_: (0,) * nd, pipeline_mode=pl.Buffered(1))


def _rope_lane_tables(pos, width):
    inv_freq = ROPE_THETA ** (-jnp.arange(ROT_HALF, dtype=F32) * 2.0 / (2 * ROT_HALF))
    ang = pos.astype(F32)[:, None] * inv_freq[None, :]
    cos, sin = jnp.cos(ang), jnp.sin(ang)
    n = pos.shape[0]
    one = jnp.ones((n, HEAD_DIM - 2 * ROT_HALF), F32)
    zero = jnp.zeros((n, HEAD_DIM - 2 * ROT_HALF), F32)
    zh = jnp.zeros((n, ROT_HALF), F32)
    c = jnp.concatenate([cos, cos, one], axis=1)
    sa = jnp.concatenate([-sin, zh, zero], axis=1)
    sb = jnp.concatenate([zh, sin, zero], axis=1)
    reps = width // HEAD_DIM
    return tuple(jnp.tile(t, (1, reps)) for t in (c, sa, sb))


def _rope_chunks(y, c, sa, sb):
    outs = []
    for j in range(y.shape[1] // LANES):
        yc = y[:, j * LANES:(j + 1) * LANES]
        outs.append(yc * c + pltpu.roll(yc, LANES - ROT_HALF, 1) * sa + pltpu.roll(yc, ROT_HALF, 1) * sb)
    return outs[0] if len(outs) == 1 else jnp.concatenate(outs, axis=1)


def _proj_kernel(*refs, wq, wkv, dil, tm, first_tile, st_rows, has_bias, kv_rep):
    if has_bias:
        x_ref, w_ref, b_ref, c_ref, sa_ref, sb_ref, q_ref, k_ref, v_ref, st_ref, scr_ref = refs
    else:
        x_ref, w_ref, c_ref, sa_ref, sb_ref, q_ref, k_ref, v_ref, st_ref, scr_ref = refs
        b_ref = None
    i = pl.program_id(1)
    xb = x_ref[0].astype(BF16)
    c, sa, sb = c_ref[...], sa_ref[...], sb_ref[...]

    def project(lo, width):
        y = jnp.dot(xb, w_ref[:, lo:lo + width], preferred_element_type=F32)
        if b_ref is not None:
            y = y + b_ref[:, lo:lo + width]
        return y

    def emit(y, out_ref):
        n = y.shape[1]
        if dil == 1:
            out_ref[0] = y.astype(BF16)
        else:
            for j in range(n // LANES):
                scr_ref[j] = y[:, j * LANES:(j + 1) * LANES]
            for r in range(dil):
                for j in range(n // LANES):
                    lo = r * n + j * LANES
                    out_ref[0, :, lo:lo + LANES] = scr_ref.at[j][pl.ds(r, tm // dil, stride=dil), :].astype(BF16)

    def replicate(y):
        lane = lax.broadcasted_iota(jnp.int32, y.shape, 1)
        swapped = pltpu.roll(y, HEAD_DIM, 1)
        h0 = jnp.where(lane < HEAD_DIM, y, swapped)
        h1 = jnp.where(lane < HEAD_DIM, swapped, y)
        return jnp.concatenate([h0, h0, h1, h1], axis=1)

    yq = _rope_chunks(project(0, wq), c, sa, sb) * SCALE
    emit(yq, q_ref)
    yk = _rope_chunks(project(wq, wkv), c, sa, sb)
    emit(replicate(yk) if kv_rep else yk, k_ref)
    yv = project(wq + wkv, wkv)
    emit(replicate(yv) if kv_rep else yv, v_ref)

    @pl.when(i >= first_tile)
    def _():
        st_ref[0, 0] = yk[tm - st_rows:, :]
        st_ref[0, 1] = yv[tm - st_rows:, :]


def _proj_prompt(x, w, bias, tabs, *, wq, wkv, dil, keep, tm, kv_rep=False):
    B, S, D = x.shape
    assert not kv_rep or (dil == 1 and wkv == N_KV_HEADS_B * HEAD_DIM == LANES)
    wkv_out = 2 * GROUP if kv_rep else wkv
    nt = S // tm
    st_rows = min(tm, keep)
    first_tile = (S - keep) // tm if keep >= tm else nt - 1
    has_bias = bias is not None
    L = S // dil

    def st_map(b, i):
        return (b, 0, jnp.maximum(i - first_tile, 0), 0)

    in_specs = [pl.BlockSpec((1, tm, D), lambda b, i: (b, i, 0)), _resident((D, wq + 2 * wkv))]
    args = [x, w]
    if has_bias:
        in_specs.append(_resident((1, wq + 2 * wkv)))
        args.append(bias)
    for t in tabs:
        in_specs.append(pl.BlockSpec((tm, LANES), lambda b, i: (i, 0)))
        args.append(t)
    out_shape = [jax.ShapeDtypeStruct((B, L, dil * wq), BF16),
                 jax.ShapeDtypeStruct((B, L, dil * wkv_out), BF16),
                 jax.ShapeDtypeStruct((B, L, dil * wkv_out), BF16),
                 jax.ShapeDtypeStruct((B, 2, keep, wkv), F32)]
    out_specs = [pl.BlockSpec((1, tm // dil, dil * wq), lambda b, i: (b, i, 0)),
                 pl.BlockSpec((1, tm // dil, dil * wkv_out), lambda b, i: (b, i, 0)),
                 pl.BlockSpec((1, tm // dil, dil * wkv_out), lambda b, i: (b, i, 0)),
                 pl.BlockSpec((1, 2, st_rows, wkv), st_map)]
    kern = functools.partial(_proj_kernel, wq=wq, wkv=wkv, dil=dil, tm=tm, first_tile=first_tile,
                             st_rows=st_rows, has_bias=has_bias, kv_rep=kv_rep)
    return pl.pallas_call(
        kern, grid=(B, nt), in_specs=in_specs, out_specs=out_specs, out_shape=out_shape,
        scratch_shapes=[pltpu.VMEM((max(wq, wkv) // LANES, tm, LANES), F32)],
        compiler_params=_cparams(("arbitrary", "arbitrary")),
        name=f"proj_prompt_d{dil}",
    )(*args)


def _head_stack(x, axis):
    head = lax.broadcasted_iota(jnp.int32, x.shape, 1) // HEAD_DIM
    zero = jnp.zeros_like(x)
    return jnp.concatenate([jnp.where(head == h, x, zero) for h in range(GROUP // HEAD_DIM)], axis=axis)


def _band_unit(q, kk, vst, mask, fills):
    nh = GROUP // HEAD_DIM
    s = lax.dot_general(_head_stack(q, 0), kk, (((1,), (1,)), ((), ())), preferred_element_type=F32)
    ps, lses = [], []
    for h in range(nh):
        sh = jnp.where(mask, s[h * BAND:(h + 1) * BAND], NEG_INF if fills is None else fills[h])
        m = jnp.max(sh, axis=-1, keepdims=True)
        e = jnp.exp(sh - m)
        den = jnp.sum(e, axis=-1, keepdims=True)
        ps.append((e * (1.0 / den)).astype(BF16))
        lses.append(m + jnp.log(den))
    p = jnp.concatenate([ps[h][:, :BAND] for h in range(nh)] + [ps[h][:, BAND:] for h in range(nh)], axis=1)
    return jnp.dot(p, vst, preferred_element_type=F32), lses


def _band_attn_kernel(*refs, R, CG, gpc, gqa, max_dist_inclusive, want_lse):
    if gqa:
        q_ref, k_ref, kp_ref, v_ref, vp_ref, sink_ref, o_ref = refs
        lse_ref = None
    else:
        q_ref, k_ref, kp_ref, v_ref, vp_ref, o_ref, lse_ref = refs
        sink_ref = None
        lse_ref[...] = jnp.zeros(lse_ref.shape, F32)
    i = pl.program_id(2)
    row = lax.broadcasted_iota(jnp.int32, (BAND, 2 * BAND), 0)
    col = lax.broadcasted_iota(jnp.int32, (BAND, 2 * BAND), 1)
    dist = BAND + row - col
    hi = BAND if max_dist_inclusive else BAND - 1
    band = (dist >= 0) & (dist <= hi)
    cur = col >= BAND

    def kv_block(ref, pref, j, c):
        src, lo = (pref, 0) if j < 0 else (ref, j * BAND)
        g = c // (CG // N_KV_HEADS_B) if gqa else c
        return src[0, lo:lo + BAND, g * GROUP:(g + 1) * GROUP]

    def drop_first_key(vs):
        key = lax.broadcasted_iota(jnp.int32, vs.shape, 0) % BAND
        return jnp.where(key == 0, jnp.zeros_like(vs), vs)

    for c in range(CG):
        fills = None
        if gqa:
            fills = [jnp.where(col == 0, sink_ref[0:1, c * 4 + h:c * 4 + h + 1], NEG_INF) for h in range(4)]
        k_prev = kv_block(k_ref, kp_ref, -1, c)
        vs_prev = _head_stack(kv_block(v_ref, vp_ref, -1, c), 0)
        for j in range(R):
            k_cur = kv_block(k_ref, kp_ref, j, c)
            vs_cur = _head_stack(kv_block(v_ref, vp_ref, j, c), 0)
            if j == 0:
                mask = band & (cur | (i > 0))
            else:
                mask = band
            q = q_ref[0, j * BAND:(j + 1) * BAND, c * GROUP:(c + 1) * GROUP]
            vst = jnp.concatenate([drop_first_key(vs_prev) if gqa else vs_prev, vs_cur], axis=0)
            o, lses = _band_unit(q, jnp.concatenate([k_prev, k_cur], axis=0), vst, mask, fills)
            o_ref[0, j * BAND:(j + 1) * BAND, c * GROUP:(c + 1) * GROUP] = o.astype(o_ref.dtype)
            if want_lse:
                for h in range(4):
                    lane = (c % gpc) * 4 + h
                    lse_ref[0, c // gpc, j * BAND:(j + 1) * BAND, lane:lane + 1] = lses[h]
            k_prev, vs_prev = k_cur, vs_cur


def _band_attn(q, k, v, sinks, *, R, CG, gpc, gqa):
    B, L, C = q.shape
    ncol = C // (CG * GROUP)
    nrow = L // (R * BAND)
    kw = k.shape[2] if gqa else CG * GROUP

    def cur_map(b, ci, i):
        return (b, i, ci)

    def kcur_map(b, ci, i):
        return (b, i, 0 if gqa else ci)

    def kprev_map(b, ci, i):
        return (b, jnp.maximum(i * R - 1, 0), 0 if gqa else ci)

    in_specs = [pl.BlockSpec((1, R * BAND, CG * GROUP), cur_map),
                pl.BlockSpec((1, R * BAND, kw), kcur_map),
                pl.BlockSpec((1, BAND, kw), kprev_map),
                pl.BlockSpec((1, R * BAND, kw), kcur_map),
                pl.BlockSpec((1, BAND, kw), kprev_map)]
    args = [q, k, k, v, v]
    out_shape = [jax.ShapeDtypeStruct((B, L, C), BF16)]
    out_specs = [pl.BlockSpec((1, R * BAND, CG * GROUP), cur_map)]
    if gqa:
        in_specs.append(_resident((1, C // HEAD_DIM)))
        args.append(sinks)
    else:
        out_shape.append(jax.ShapeDtypeStruct((B, C // (gpc * GROUP), L, LANES), F32))
        out_specs.append(pl.BlockSpec((1, CG // gpc, R * BAND, LANES), lambda b, ci, i: (b, ci, i, 0)))
    kern = functools.partial(_band_attn_kernel, R=R, CG=CG, gpc=gpc, gqa=gqa, max_dist_inclusive=not gqa,
                             want_lse=not gqa)
    return pl.pallas_call(
        kern, grid=(B, ncol, nrow), in_specs=in_specs, out_specs=out_specs, out_shape=out_shape,
        compiler_params=_cparams(("arbitrary", "arbitrary", "arbitrary")),
        name="band_attn_gqa" if gqa else f"band_attn_c{C}",
    )(*args)


def _layer_norm(z, g, b):
    mu = jnp.mean(z, axis=-1, keepdims=True)
    zc = z - mu
    var = jnp.mean(zc * zc, axis=-1, keepdims=True)
    return zc * lax.rsqrt(var + LN_EPS) * g + b


def _oproj_kernel(*refs, dils, tm, alpha, has_bias):
    nb = len(dils)
    o_refs = refs[:nb]
    pos = nb
    lse_refs = ()
    if nb > 1:
        lse_refs = refs[pos:pos + nb]
        pos += nb
        e_ref = refs[pos]
        pos += 1
    x_ref, w_ref = refs[pos], refs[pos + 1]
    pos += 2
    b_ref = None
    if has_bias:
        b_ref = refs[pos]
        pos += 1
    g_ref, beta_ref, out_ref = refs[pos], refs[pos + 1], refs[pos + 2]
    scr = refs[pos + 3:]

    if nb == 1:
        merged = o_refs[0][0].astype(BF16)
    else:
        os_, ls = [], []
        si = 0
        for n, dil in enumerate(dils):
            if dil == 1:
                os_.append(o_refs[n][0].astype(F32))
                ls.append(lse_refs[n][0, 0])
            else:
                so, sl = scr[si], scr[si + 1]
                si += 2
                nch = so.shape[0]
                for r in range(dil):
                    for jc in range(nch):
                        lo = (r * nch + jc) * LANES
                        so.at[jc][pl.ds(r, tm // dil, stride=dil), :] = o_refs[n][0, :, lo:lo + LANES].astype(F32)
                    sl[pl.ds(r, tm // dil, stride=dil), :] = lse_refs[n][0, r]
                os_.append(jnp.concatenate([so[jc] for jc in range(nch)], axis=1))
                ls.append(sl[...])
        m = functools.reduce(jnp.maximum, ls)
        ws = [jnp.exp(l - m) for l in ls]
        tot = functools.reduce(lambda a, b: a + b, ws)
        acc = None
        for o, w_ in zip(os_, ws):
            w_ = w_ / tot
            w_hi = w_.astype(BF16)
            w_lo = (w_ - w_hi.astype(F32)).astype(BF16)
            wexp = (jnp.dot(w_hi, e_ref[...], preferred_element_type=F32)
                    + jnp.dot(w_lo, e_ref[...], preferred_element_type=F32))
            acc = wexp * o if acc is None else acc + wexp * o
        merged = acc.astype(BF16)
    y = jnp.dot(merged, w_ref[...], preferred_element_type=F32)
    if b_ref is not None:
        y = y + b_ref[...]
    z = alpha * x_ref[0] + y
    out_ref[0] = _layer_norm(z, g_ref[...], beta_ref[...])


def _oproj_ln(os_, lses, x, w, bias, g, beta, *, dils, tm, alpha):
    B, S, D = x.shape
    C = w.shape[0]
    nb = len(dils)
    in_specs, args, scratch = [], [], []
    for o, dil in zip(os_, dils):
        in_specs.append(pl.BlockSpec((1, tm // dil, dil * C), lambda b, i: (b, i, 0)))
        args.append(o)
    if nb > 1:
        for l, dil in zip(lses, dils):
            in_specs.append(pl.BlockSpec((1, dil, tm // dil, LANES), lambda b, i: (b, 0, i, 0)))
            args.append(l)
            if dil > 1:
                scratch += [pltpu.VMEM((C // LANES, tm, LANES), F32), pltpu.VMEM((tm, LANES), F32)]
        expand = (jnp.arange(C)[None, :] // HEAD_DIM == jnp.arange(LANES)[:, None]).astype(BF16)
        in_specs.append(_resident((LANES, C)))
        args.append(expand)
    in_specs += [pl.BlockSpec((1, tm, D), lambda b, i: (b, i, 0)), _resident((C, D))]
    args += [x, w]
    if bias is not None:
        in_specs.append(_resident((1, D)))
        args.append(bias)
    in_specs += [_resident((1, D))] * 2
    args += [g, beta]
    kern = functools.partial(_oproj_kernel, dils=dils, tm=tm, alpha=alpha, has_bias=bias is not None)
    return pl.pallas_call(
        kern, grid=(B, S // tm), in_specs=in_specs,
        out_specs=pl.BlockSpec((1, tm, D), lambda b, i: (b, i, 0)),
        out_shape=jax.ShapeDtypeStruct((B, S, D), F32),
        scratch_shapes=scratch,
        compiler_params=_cparams(("arbitrary", "arbitrary")),
        name=f"oproj_ln_{nb}",
    )(*args)


FF_CHUNK = 256
FFN_TM = 1024


def _ffn_kernel(*refs, tm, ff, alpha, seq_conv):
    if seq_conv:
        x_ref, wup_ref, cw_ref, cb_ref, wdn_ref, g_ref, beta_ref, out_ref, st_ref, g_scr, carry_scr, act_scr = refs
    else:
        (x_ref, p0_ref, p1_ref, wup_ref, cw_ref, cb_ref, wdn_ref, g_ref, beta_ref,
         out_ref, st_ref, act_scr) = refs
    i = pl.program_id(1)
    last = pl.num_programs(1) - 1
    x = x_ref[0]
    xb = x.astype(BF16)
    for c in range(ff // FF_CHUNK):
        lo = c * FF_CHUNK
        gate = jnp.dot(xb, wup_ref[:, lo:lo + FF_CHUNK], preferred_element_type=F32)
        up = jnp.dot(xb, wup_ref[:, ff + lo:ff + lo + FF_CHUNK], preferred_element_type=F32)
        if seq_conv:
            carry = carry_scr[:, lo:lo + FF_CHUNK]
            g_scr[0:8, :] = jnp.where(i > 0, carry, jnp.zeros_like(carry))
            g_scr[8:8 + tm, :] = gate
            g1 = g_scr[7:7 + tm, :]
            g2 = g_scr[6:6 + tm, :]
            carry_scr[:, lo:lo + FF_CHUNK] = gate[tm - 8:, :]

            @pl.when(i == last)
            def _():
                st_ref[0, :, lo:lo + FF_CHUNK] = gate[tm - (CONV_W - 1):, :]
        else:
            g2 = p0_ref[:, lo:lo + FF_CHUNK]
            g1 = p1_ref[:, lo:lo + FF_CHUNK]
            st_ref[0, :, lo:lo + FF_CHUNK] = gate
        cw = cw_ref[:, lo:lo + FF_CHUNK]
        gc = cb_ref[:, lo:lo + FF_CHUNK] + cw[0:1] * g2 + cw[1:2] * g1 + cw[2:3] * gate
        act = gc * jax.nn.sigmoid(gc) * up
        act_scr[:, lo:lo + FF_CHUNK] = act.astype(BF16)
    y = jnp.dot(act_scr[...], wdn_ref[...], preferred_element_type=F32)
    out_ref[0] = _layer_norm(alpha * x + y, g_ref[...], beta_ref[...])


def _ffn_ln(x, prev, w_up, cw, cb, w_dn, g, beta, *, tm, alpha):
    B, S, D = x.shape
    ff = w_dn.shape[0]
    seq_conv = prev is None
    in_specs = [pl.BlockSpec((1, tm, D), lambda b, i: (b, i, 0))]
    args = [x]
    if not seq_conv:
        in_specs += [pl.BlockSpec((tm, ff), lambda b, i: (i, 0))] * 2
        args += list(prev)
    in_specs += [_resident((D, 2 * ff)), _resident((CONV_W, ff)), _resident((1, ff)), _resident((ff, D)),
                 _resident((1, D)), _resident((1, D))]
    args += [w_up, cw, cb, w_dn, g, beta]
    if seq_conv:
        st_shape = jax.ShapeDtypeStruct((B, CONV_W - 1, ff), F32)
        st_spec = pl.BlockSpec((1, CONV_W - 1, ff), lambda b, i: (b, 0, 0))
        scratch = [pltpu.VMEM((tm + 8, FF_CHUNK), F32), pltpu.VMEM((8, ff), F32), pltpu.VMEM((tm, ff), BF16)]
    else:
        st_shape = jax.ShapeDtypeStruct((B, S, ff), F32)
        st_spec = pl.BlockSpec((1, tm, ff), lambda b, i: (b, i, 0))
        scratch = [pltpu.VMEM((tm, ff), BF16)]
    kern = functools.partial(_ffn_kernel, tm=tm, ff=ff, alpha=alpha, seq_conv=seq_conv)
    return pl.pallas_call(
        kern, grid=(B, S // tm), in_specs=in_specs,
        out_specs=[pl.BlockSpec((1, tm, D), lambda b, i: (b, i, 0)), st_spec],
        out_shape=[jax.ShapeDtypeStruct((B, S, D), F32), st_shape],
        scratch_shapes=scratch,
        compiler_params=_cparams(("arbitrary", "arbitrary")),
        name="ffn_ln_seq" if seq_conv else "ffn_ln_rows",
    )(*args)


def _proj_sample_kernel(*refs, has_bias):
    if has_bias:
        x_ref, w_ref, b_ref, c_ref, sa_ref, sb_ref, nat_ref, tr_ref = refs
    else:
        x_ref, w_ref, c_ref, sa_ref, sb_ref, nat_ref, tr_ref = refs
    y = jnp.dot(x_ref[...].astype(BF16), w_ref[...], preferred_element_type=F32)
    if has_bias:
        y = y + b_ref[...]
    outs = []
    for j in range(y.shape[1] // LANES):
        sl = slice(j * LANES, (j + 1) * LANES)
        yc = y[:, sl]
        outs.append(yc * c_ref[:, sl] + pltpu.roll(yc, LANES - ROT_HALF, 1) * sa_ref[:, sl]
                    + pltpu.roll(yc, ROT_HALF, 1) * sb_ref[:, sl])
    y = jnp.concatenate(outs, axis=1)
    nat_ref[0] = y
    tr_ref[0] = y.T


def _proj_sample(x, w, bias, tabs, gw):
    N, D = x.shape
    n = w.shape[1] // gw
    in_specs = [pl.BlockSpec((N, D), lambda j: (0, 0)), pl.BlockSpec((D, gw), lambda j: (0, j))]
    args = [x, w]
    if bias is not None:
        in_specs.append(pl.BlockSpec((1, gw), lambda j: (0, j)))
        args.append(bias)
    in_specs += [pl.BlockSpec((1, gw), lambda j: (0, j))] * 3
    args += list(tabs)
    return pl.pallas_call(
        functools.partial(_proj_sample_kernel, has_bias=bias is not None),
        grid=(n,), in_specs=in_specs,
        out_specs=[pl.BlockSpec((1, N, gw), lambda j: (j, 0, 0)), pl.BlockSpec((1, gw, N), lambda j: (j, 0, 0))],
        out_shape=[jax.ShapeDtypeStruct((n, N, gw), F32), jax.ShapeDtypeStruct((n, gw, N), F32)],
        compiler_params=_cparams(("arbitrary",)),
        name="proj_sample",
    )(*args)


def _pick_column(x, b):
    lane = lax.broadcasted_iota(jnp.int32, x.shape, 1)
    return jnp.sum(jnp.where(lane == b, x, 0.0), axis=1, keepdims=True)


def _lane_chunks(x):
    return [x[:, j * LANES:(j + 1) * LANES] for j in range(x.shape[1] // LANES)]


def _tree(op, xs):
    while len(xs) > 1:
        xs = [op(xs[i], xs[i + 1]) if i + 1 < len(xs) else xs[i] for i in range(0, len(xs), 2)]
    return xs[0]


def _roll_in(buf_t, new_col):
    L = buf_t.shape[1]
    lane = lax.broadcasted_iota(jnp.int32, buf_t.shape, 1)
    return jnp.where(lane == L - 1, new_col, pltpu.roll(buf_t, L - 1, 1))


def _sample_attn_a_kernel(qkv_ref, c1_ref, c2_ref, c3_ref, n1_ref, n2_ref, n3_ref, ot_ref, *, hpc):
    b = pl.program_id(0)
    hc = pl.program_id(1)
    rows = hpc * HEAD_DIM
    cols = [_pick_column(qkv_ref[t], b) for t in range(3 * len(DILATIONS))]
    nb = ot_ref.shape[1]
    lane_b = lax.broadcasted_iota(jnp.int32, (HEAD_DIM, nb), 1)

    @pl.when(b == 0)
    def _():
        ot_ref[pl.ds(pl.multiple_of(hc * rows, rows), rows), :] = jnp.zeros((rows, nb), F32)

    for hh in range(hpc):
        hs = slice(hh * HEAD_DIM, (hh + 1) * HEAD_DIM)
        branches = list(zip(DILATIONS, (c1_ref, c2_ref, c3_ref), (n1_ref, n2_ref, n3_ref)))
        ss, s_news = [], []
        for g, (dil, c_ref, _) in enumerate(branches):
            qc, kc = cols[3 * g][hs], cols[3 * g + 1][hs]
            kt = c_ref[0, 0, hh]
            lane = lax.broadcasted_iota(jnp.int32, (1, kt.shape[1]), 1)
            s = jnp.sum(qc * kt, axis=0, keepdims=True)
            ss.append(jnp.where(lane % dil == 0, s, NEG_INF))
            s_news.append(jnp.sum(qc * kc, axis=0, keepdims=True))
        m = jnp.max(_tree(jnp.maximum, [c for s in ss for c in _lane_chunks(s)]), axis=1, keepdims=True)
        m = _tree(jnp.maximum, [m] + s_news)
        es = [jnp.exp(s - m) for s in ss]
        e_news = [jnp.exp(s - m) for s in s_news]
        den = jnp.sum(_tree(jnp.add, [c for e in es for c in _lane_chunks(e)]), axis=1, keepdims=True)
        den = _tree(jnp.add, [den] + e_news)
        pv, o_new = [], []
        for g, (dil, c_ref, n_ref) in enumerate(branches):
            kc, vc = cols[3 * g + 1][hs], cols[3 * g + 2][hs]
            vt = c_ref[0, 1, hh]
            pv += _lane_chunks(es[g] * vt)
            o_new.append(e_news[g] * vc)
            n_ref[0, 0, hh] = _roll_in(c_ref[0, 0, hh], kc)
            n_ref[0, 1, hh] = _roll_in(vt, vc)
        o = (jnp.sum(_tree(jnp.add, pv), axis=1, keepdims=True) + _tree(jnp.add, o_new)) * (1.0 / den)
        r0 = pl.multiple_of(hc * rows + hh * HEAD_DIM, HEAD_DIM)
        old = ot_ref[pl.ds(r0, HEAD_DIM), :]
        ot_ref[pl.ds(r0, HEAD_DIM), :] = jnp.where(lane_b == b, o, old)


def _sample_attn_a(qkv_t, caches, *, hpc):
    _, C, N = qkv_t.shape
    H = C // HEAD_DIM

    def cspec(c):
        L = c.shape[-1]
        return pl.BlockSpec((1, 2, hpc, HEAD_DIM, L), lambda b, hc: (b, 0, hc, 0, 0))

    in_specs = [pl.BlockSpec((qkv_t.shape[0], hpc * HEAD_DIM, N), lambda b, hc: (0, hc, 0))]
    in_specs += [cspec(c) for c in caches]
    out_specs = [cspec(c) for c in caches] + [pl.BlockSpec((C, N), lambda b, hc: (0, 0))]
    out_shape = [jax.ShapeDtypeStruct(c.shape, F32) for c in caches] + [jax.ShapeDtypeStruct((C, N), F32)]
    return pl.pallas_call(
        functools.partial(_sample_attn_a_kernel, hpc=hpc),
        grid=(N, H // hpc), in_specs=in_specs, out_specs=out_specs, out_shape=out_shape,
        compiler_params=_cparams(("arbitrary", "arbitrary")),
        name="sample_attn_a",
    )(qkv_t, *caches)


SAMPLE_B_PER_STEP = 8


def _sample_attn_b_kernel(q_ref, k_ref, v_ref, sink_ref, c_ref, n_ref, o_ref, *, n_heads, nbs):
    gqa = n_heads // N_KV_HEADS_B
    for bi in range(nbs):
        for kv in range(N_KV_HEADS_B):
            ds_ = slice(kv * HEAD_DIM, (kv + 1) * HEAD_DIM)
            hs = slice(kv * gqa, (kv + 1) * gqa)
            kt = _roll_in(c_ref[bi, 0, kv], k_ref[bi, ds_, :])
            vt = _roll_in(c_ref[bi, 1, kv], v_ref[bi, ds_, :])
            n_ref[bi, 0, kv] = kt
            n_ref[bi, 1, kv] = vt
            L = kt.shape[1]
            lane = lax.broadcasted_iota(jnp.int32, (1, L), 1)
            q = q_ref[bi, hs, :].astype(BF16)
            s = jnp.dot(q, kt.astype(BF16), preferred_element_type=F32)
            s = jnp.where(L - 1 - lane < WINDOW_B, s, NEG_INF)
            sink = sink_ref[hs, :]
            m = jnp.maximum(jnp.max(s, axis=1, keepdims=True), sink)
            e = jnp.exp(s - m)
            den = jnp.sum(e, axis=1, keepdims=True) + jnp.exp(sink - m)
            p = (e * (1.0 / den)).astype(BF16)
            o_ref[bi, hs, :] = lax.dot_general(p, vt.astype(BF16), (((1,), (1,)), ((), ())),
                                               preferred_element_type=F32)


def _sample_attn_b(q3, k_new, v_new, sinks, cache):
    N, H, _ = q3.shape
    nbs = math.gcd(SAMPLE_B_PER_STEP, N)

    def step(shape):
        return pl.BlockSpec((nbs,) + shape[1:], lambda b: (b,) + (0,) * (len(shape) - 1))

    return pl.pallas_call(
        functools.partial(_sample_attn_b_kernel, n_heads=H, nbs=nbs),
        grid=(N // nbs,),
        in_specs=[step(q3.shape), step(k_new.shape), step(v_new.shape), _resident(sinks.shape), step(cache.shape)],
        out_specs=[step(cache.shape), step(q3.shape)],
        out_shape=[jax.ShapeDtypeStruct(cache.shape, F32), jax.ShapeDtypeStruct(q3.shape, F32)],
        compiler_params=_cparams(("arbitrary",)),
        name="sample_attn_b",
    )(q3, k_new, v_new, sinks, cache)


def _to_positions_minor(c):
    return jnp.transpose(c, (0, 1, 3, 4, 2))


def _from_positions_minor(c):
    return jnp.transpose(c, (0, 1, 4, 2, 3))


def kernel(x_prompt, x_sample, cache_a1, cache_a2, cache_a3, cache_b, state_ffn, w_qkv_a, w_o_a, w_qkv_b,
           b_qkv_b, sinks_b, w_o_b, b_o_b, w_up, conv_w, conv_b, w_down, ln_g, ln_b):
    B, S, D = x_prompt.shape
    N, T, _ = x_sample.shape
    assert T == 1, "the sample group is decoded one token per request"
    depth = w_up.shape[0]
    ff = w_down.shape[1]
    alpha = (2.0 * depth) ** 0.25
    H = D // HEAD_DIM
    gpc = D // GROUP
    nq_b = w_o_b.shape[1]
    kvw_b = N_KV_HEADS_B * HEAD_DIM
    caches_a = (cache_a1, cache_a2, cache_a3)
    tm = min(512, S)

    pos_p = jnp.arange(S, dtype=jnp.int32)
    tabs_p = _rope_lane_tables(pos_p, LANES)
    pos_s = PAST_LEN + jnp.arange(T, dtype=jnp.int32)
    c_s, sa_s, sb_s = _rope_lane_tables(pos_s, D)

    xp = x_prompt
    xs = x_sample.reshape(1, N, D)
    new_a_p = [[] for _ in DILATIONS]
    new_a_s = [[] for _ in DILATIONS]
    new_b_p, new_b_s, new_f_p, new_f_s = [], [], [], []

    for i in range(depth):
        j = i // 2
        g1, b1 = ln_g[i, 0][None], ln_b[i, 0][None]
        g2, b2 = ln_g[i, 1][None], ln_b[i, 1][None]
        if i % 2 == 0:
            w_all = w_qkv_a[j].astype(BF16)
            wo = w_o_a[j].astype(BF16)
            os_, lses = [], []
            for g, dil in enumerate(DILATIONS):
                q, k, v, st = _proj_prompt(
                    xp, w_all[:, g * 3 * D:(g + 1) * 3 * D], None, tabs_p,
                    wq=D, wkv=D, dil=dil, keep=min(BAND * dil, S), tm=tm)
                R = min(4, (S // dil) // BAND)
                cg = gpc * max(1, min(dil, (UNITS_PER_STEP // R) // gpc))
                o, lse = _band_attn(q, k, v, None, R=R, CG=cg, gpc=gpc, gqa=False)
                os_.append(o)
                lses.append(lse)
                new_a_p[g].append(st.reshape(B, 2, st.shape[2], H, HEAD_DIM))
            hp = _oproj_ln(os_, lses, xp, wo, None, g1, b1, dils=DILATIONS, tm=tm, alpha=alpha)
            one = jnp.ones((1, D), F32)
            zero = jnp.zeros((1, D), F32)
            c_tab = jnp.concatenate([c_s * SCALE, c_s, one] * len(DILATIONS), axis=1)
            sa_tab = jnp.concatenate([sa_s * SCALE, sa_s, zero] * len(DILATIONS), axis=1)
            sb_tab = jnp.concatenate([sb_s * SCALE, sb_s, zero] * len(DILATIONS), axis=1)
            nat, tr = _proj_sample(xs[0], w_all, None, (c_tab, sa_tab, sb_tab), D)
            outs = _sample_attn_a(tr, [_to_positions_minor(c[j]) for c in caches_a], hpc=min(4, H))
            for g in range(len(DILATIONS)):
                new_a_s[g].append(_from_positions_minor(outs[g]))
            hs = _oproj_ln([outs[-1].T[None]], None, xs, wo, None, g1, b1, dils=(1,), tm=N, alpha=alpha)
        else:
            w_all = w_qkv_b[j].astype(BF16)
            wo = w_o_b[j].astype(BF16)
            bias = b_qkv_b[j][None]
            sinks = sinks_b[j][None]
            q, k, v, st = _proj_prompt(xp, w_all, bias, tabs_p, wq=nq_b, wkv=kvw_b, dil=1,
                                       keep=min(WINDOW_B, S), tm=tm, kv_rep=True)
            (o,) = _band_attn(q, k, v, sinks, R=min(4, S // BAND), CG=nq_b // GROUP, gpc=nq_b // GROUP, gqa=True)
            new_b_p.append(st.reshape(B, 2, st.shape[2], N_KV_HEADS_B, HEAD_DIM))
            hp = _oproj_ln([o], None, xp, wo, b_o_b[j][None], g1, b1, dils=(1,), tm=tm, alpha=alpha)
            nw = w_all.shape[1]
            reps = nw // D + 1
            c_full = jnp.concatenate([c_s * SCALE] + [c_s] * reps, axis=1)
            sa_full = jnp.concatenate([sa_s * SCALE] + [sa_s] * reps, axis=1)
            sb_full = jnp.concatenate([sb_s * SCALE] + [sb_s] * reps, axis=1)
            is_v = (jnp.arange(nw) >= nq_b + kvw_b)[None]
            c_tab = jnp.where(is_v, 1.0, c_full[:, :nw])
            sa_tab = jnp.where(is_v, 0.0, sa_full[:, :nw])
            sb_tab = jnp.where(is_v, 0.0, sb_full[:, :nw])
            nat, tr = _proj_sample(xs[0], w_all, bias, (c_tab, sa_tab, sb_tab), nw)
            nat = nat[0]
            new_c, o3 = _sample_attn_b(nat[:, :nq_b].reshape(N, nq_b // HEAD_DIM, HEAD_DIM),
                                       nat[:, nq_b:nq_b + kvw_b].reshape(N, kvw_b, 1),
                                       nat[:, nq_b + kvw_b:].reshape(N, kvw_b, 1),
                                       sinks_b[j][:, None], _to_positions_minor(cache_b[j]))
            new_b_s.append(_from_positions_minor(new_c))
            hs = _oproj_ln([o3.reshape(1, N, nq_b)], None, xs, wo, b_o_b[j][None], g1, b1, dils=(1,), tm=N,
                           alpha=alpha)
        wup = w_up[i].astype(BF16)
        wdn = w_down[i].astype(BF16)
        cb = conv_b[i][None]
        xp, st_p = _ffn_ln(hp, None, wup, conv_w[i], cb, wdn, g2, b2, tm=min(FFN_TM, S), alpha=alpha)
        xs, g_s = _ffn_ln(hs, (state_ffn[i][:, 0], state_ffn[i][:, 1]), wup, conv_w[i], cb, wdn, g2, b2,
                          tm=N, alpha=alpha)
        new_f_p.append(st_p)
        new_f_s.append(jnp.stack([state_ffn[i][:, 1], g_s[0]], axis=1))

    return (xp, xs.reshape(N, T, D),
            jnp.stack(new_a_p[0]), jnp.stack(new_a_s[0]),
            jnp.stack(new_a_p[1]), jnp.stack(new_a_s[1]),
            jnp.stack(new_a_p[2]), jnp.stack(new_a_s[2]),
            jnp.stack(new_b_p), jnp.stack(new_b_s),
            jnp.stack(new_f_p), jnp.stack(new_f_s))
```

```python
import functools
import math

import jax
import jax.numpy as jnp
from jax import lax
from jax.experimental import pallas as pl
from jax.experimental.pallas import tpu as pltpu

F32 = jnp.float32
BF16 = jnp.bfloat16

HEAD_DIM = 64
ROT_HALF = HEAD_DIM // 8
ROPE_THETA = 500000.0
DILATIONS = (1, 4, 16)
BAND = 128
WINDOW_B = 128
N_KV_HEADS_B = 2
PAST_LEN = 16384
CONV_W = 3
LN_EPS = 1e-5
NEG_INF = -1e30
SCALE = 1.0 / math.sqrt(HEAD_DIM)

LANES = 128
GROUP = 4 * HEAD_DIM
UNITS_PER_STEP = 32
SAMPLE_A_HEADS_PER_STEP = 8
VMEM_LIMIT = 56 * 1024 * 1024


def _cparams(sem):
    return pltpu.CompilerParams(dimension_semantics=sem, vmem_limit_bytes=VMEM_LIMIT)


def _resident(shape):
    nd = len(shape)
    return pl.BlockSpec(shape, lambda *_: (0,) * nd, pipeline_mode=pl.Buffered(1))


def _rope_lane_tables(pos, width):
    inv_freq = ROPE_THETA ** (-jnp.arange(ROT_HALF, dtype=F32) * 2.0 / (2 * ROT_HALF))
    ang = pos.astype(F32)[:, None] * inv_freq[None, :]
    cos, sin = jnp.cos(ang), jnp.sin(ang)
    n = pos.shape[0]
    one = jnp.ones((n, HEAD_DIM - 2 * ROT_HALF), F32)
    zero = jnp.zeros((n, HEAD_DIM - 2 * ROT_HALF), F32)
    zh = jnp.zeros((n, ROT_HALF), F32)
    c = jnp.concatenate([cos, cos, one], axis=1)
    sa = jnp.concatenate([-sin, zh, zero], axis=1)
    sb = jnp.concatenate([zh, sin, zero], axis=1)
    reps = width // HEAD_DIM
    return tuple(jnp.tile(t, (1, reps)) for t in (c, sa, sb))


def _rope_chunks(y, c, sa, sb):
    outs = []
    for j in range(y.shape[1] // LANES):
        yc = y[:, j * LANES:(j + 1) * LANES]
        outs.append(yc * c + pltpu.roll(yc, LANES - ROT_HALF, 1) * sa + pltpu.roll(yc, ROT_HALF, 1) * sb)
    return outs[0] if len(outs) == 1 else jnp.concatenate(outs, axis=1)


def _proj_kernel(*refs, wq, wkv, dil, tm, first_tile, st_rows, has_bias, kv_rep):
    refs = list(refs)
    x_ref, w_ref = refs.pop(0), refs.pop(0)
    b_ref = refs.pop(0) if has_bias else None
    tabs = [refs.pop(0)[...] for _ in range(3)]
    tabs_cm = [refs.pop(0)[...] for _ in range(3)] if dil > 1 else tabs
    q_ref, k_ref, v_ref, st_ref, scr_ref, xb_scr = refs
    i = pl.program_id(1)
    rows = tm // dil
    if dil == 1:
        xb = x_ref[0].astype(BF16)
    else:
        for j in range(x_ref.shape[2] // LANES):
            scr_ref[j] = x_ref[0, :, j * LANES:(j + 1) * LANES]
        for r in range(dil):
            for j in range(x_ref.shape[2] // LANES):
                xb_scr[r * rows:(r + 1) * rows, j * LANES:(j + 1) * LANES] = (
                    scr_ref.at[j][pl.ds(r, rows, stride=dil), :].astype(BF16))
        xb = xb_scr[...]

    def project(lhs, lo, width):
        y = jnp.dot(lhs, w_ref[:, lo:lo + width], preferred_element_type=F32)
        if b_ref is not None:
            y = y + b_ref[:, lo:lo + width]
        return y

    def emit(y, out_ref):
        n = y.shape[1]
        for r in range(dil):
            out_ref[0, :, r * n:(r + 1) * n] = y[r * rows:(r + 1) * rows, :].astype(BF16)

    def replicate(y):
        lane = lax.broadcasted_iota(jnp.int32, y.shape, 1)
        swapped = pltpu.roll(y, HEAD_DIM, 1)
        h0 = jnp.where(lane < HEAD_DIM, y, swapped)
        h1 = jnp.where(lane < HEAD_DIM, swapped, y)
        return jnp.concatenate([h0, h0, h1, h1], axis=1)

    yq = _rope_chunks(project(xb, 0, wq), *tabs_cm) * SCALE
    emit(yq, q_ref)
    yk = _rope_chunks(project(xb, wq, wkv), *tabs_cm)
    emit(replicate(yk) if kv_rep else yk, k_ref)
    yv = project(xb, wq + wkv, wkv)
    emit(replicate(yv) if kv_rep else yv, v_ref)

    @pl.when(i >= first_tile)
    def _():
        if dil == 1:
            yk_seq, yv_seq = yk, yv
        else:
            xn = x_ref[0].astype(BF16)
            yk_seq = _rope_chunks(project(xn, wq, wkv), *tabs)
            yv_seq = project(xn, wq + wkv, wkv)
        st_ref[0, 0] = yk_seq[tm - st_rows:, :]
        st_ref[0, 1] = yv_seq[tm - st_rows:, :]


def _proj_prompt(x, w, bias, tabs, *, wq, wkv, dil, keep, tm, kv_rep=False):
    B, S, D = x.shape
    assert not kv_rep or (dil == 1 and wkv == N_KV_HEADS_B * HEAD_DIM == LANES)
    wkv_out = 2 * GROUP if kv_rep else wkv
    nt = S // tm
    st_rows = min(tm, keep)
    first_tile = (S - keep) // tm if keep >= tm else nt - 1
    has_bias = bias is not None
    L = S // dil

    def st_map(b, i):
        return (b, 0, jnp.maximum(i - first_tile, 0), 0)

    in_specs = [pl.BlockSpec((1, tm, D), lambda b, i: (b, i, 0)), _resident((D, wq + 2 * wkv))]
    args = [x, w]
    if has_bias:
        in_specs.append(_resident((1, wq + 2 * wkv)))
        args.append(bias)
    tabs = list(tabs)
    if dil > 1:
        tabs += [t.reshape(nt, tm // dil, dil, LANES).transpose(0, 2, 1, 3).reshape(S, LANES) for t in tabs]
    for t in tabs:
        in_specs.append(pl.BlockSpec((tm, LANES), lambda b, i: (i, 0)))
        args.append(t)
    out_shape = [jax.ShapeDtypeStruct((B, L, dil * wq), BF16),
                 jax.ShapeDtypeStruct((B, L, dil * wkv_out), BF16),
                 jax.ShapeDtypeStruct((B, L, dil * wkv_out), BF16),
                 jax.ShapeDtypeStruct((B, 2, keep, wkv), F32)]
    out_specs = [pl.BlockSpec((1, tm // dil, dil * wq), lambda b, i: (b, i, 0)),
                 pl.BlockSpec((1, tm // dil, dil * wkv_out), lambda b, i: (b, i, 0)),
                 pl.BlockSpec((1, tm // dil, dil * wkv_out), lambda b, i: (b, i, 0)),
                 pl.BlockSpec((1, 2, st_rows, wkv), st_map)]
    kern = functools.partial(_proj_kernel, wq=wq, wkv=wkv, dil=dil, tm=tm, first_tile=first_tile,
                             st_rows=st_rows, has_bias=has_bias, kv_rep=kv_rep)
    return pl.pallas_call(
        kern, grid=(B, nt), in_specs=in_specs, out_specs=out_specs, out_shape=out_shape,
        scratch_shapes=([pltpu.VMEM((D // LANES, tm, LANES), F32), pltpu.VMEM((tm, D), BF16)] if dil > 1 else
                        [pltpu.VMEM((1, 8, LANES), F32), pltpu.VMEM((16, LANES), BF16)]),
        compiler_params=_cparams(("arbitrary", "arbitrary")),
        name=f"proj_prompt_d{dil}",
    )(*args)


def _head_stack(x, axis):
    head = lax.broadcasted_iota(jnp.int32, x.shape, 1) // HEAD_DIM
    zero = jnp.zeros_like(x)
    return jnp.concatenate([jnp.where(head == h, x, zero) for h in range(GROUP // HEAD_DIM)], axis=axis)


def _band_unit(q, kk, vst, mask, fills):
    nh = GROUP // HEAD_DIM
    s = lax.dot_general(_head_stack(q, 0), kk, (((1,), (1,)), ((), ())), preferred_element_type=F32)
    ps, lses = [], []
    for h in range(nh):
        sh = jnp.where(mask, s[h * BAND:(h + 1) * BAND], NEG_INF if fills is None else fills[h])
        m = jnp.max(sh, axis=-1, keepdims=True)
        e = jnp.exp(sh - m)
        den = jnp.sum(e, axis=-1, keepdims=True)
        ps.append((e * (1.0 / den)).astype(BF16))
        lses.append(m + jnp.log(den))
    p = jnp.concatenate([ps[h][:, :BAND] for h in range(nh)] + [ps[h][:, BAND:] for h in range(nh)], axis=1)
    return jnp.dot(p, vst, preferred_element_type=F32), lses


def _band_attn_kernel(*refs, R, CG, gpc, gqa, max_dist_inclusive, want_lse):
    if gqa:
        q_ref, k_ref, kp_ref, v_ref, vp_ref, sink_ref, o_ref = refs
        lse_ref = None
    else:
        q_ref, k_ref, kp_ref, v_ref, vp_ref, o_ref, lse_ref = refs
        sink_ref = None
        lse_ref[...] = jnp.zeros(lse_ref.shape, F32)
    i = pl.program_id(2)
    row = lax.broadcasted_iota(jnp.int32, (BAND, 2 * BAND), 0)
    col = lax.broadcasted_iota(jnp.int32, (BAND, 2 * BAND), 1)
    dist = BAND + row - col
    hi = BAND if max_dist_inclusive else BAND - 1
    band = (dist >= 0) & (dist <= hi)
    cur = col >= BAND

    def kv_block(ref, pref, j, c):
        src, lo = (pref, 0) if j < 0 else (ref, j * BAND)
        g = c // (CG // N_KV_HEADS_B) if gqa else c
        return src[0, lo:lo + BAND, g * GROUP:(g + 1) * GROUP]

    def drop_first_key(vs):
        key = lax.broadcasted_iota(jnp.int32, vs.shape, 0) % BAND
        return jnp.where(key == 0, jnp.zeros_like(vs), vs)

    for c in range(CG):
        fills = None
        if gqa:
            fills = [jnp.where(col == 0, sink_ref[0:1, c * 4 + h:c * 4 + h + 1], NEG_INF) for h in range(4)]
        k_prev = kv_block(k_ref, kp_ref, -1, c)
        vs_prev = _head_stack(kv_block(v_ref, vp_ref, -1, c), 0)
        for j in range(R):
            k_cur = kv_block(k_ref, kp_ref, j, c)
            vs_cur = _head_stack(kv_block(v_ref, vp_ref, j, c), 0)
            if j == 0:
                mask = band & (cur | (i > 0))
            else:
                mask = band
            q = q_ref[0, j * BAND:(j + 1) * BAND, c * GROUP:(c + 1) * GROUP]
            vst = jnp.concatenate([drop_first_key(vs_prev) if gqa else vs_prev, vs_cur], axis=0)
            o, lses = _band_unit(q, jnp.concatenate([k_prev, k_cur], axis=0), vst, mask, fills)
            o_ref[0, j * BAND:(j + 1) * BAND, c * GROUP:(c + 1) * GROUP] = o.astype(o_ref.dtype)
            if want_lse:
                for h in range(4):
                    lane = (c % gpc) * 4 + h
                    lse_ref[0, c // gpc, j * BAND:(j + 1) * BAND, lane:lane + 1] = lses[h]
            k_prev, vs_prev = k_cur, vs_cur


def _band_attn(q, k, v, sinks, *, R, CG, gpc, gqa):
    B, L, C = q.shape
    ncol = C // (CG * GROUP)
    nrow = L // (R * BAND)
    kw = k.shape[2] if gqa else CG * GROUP

    def cur_map(b, ci, i):
        return (b, i, ci)

    def kcur_map(b, ci, i):
        return (b, i, 0 if gqa else ci)

    def kprev_map(b, ci, i):
        return (b, jnp.maximum(i * R - 1, 0), 0 if gqa else ci)

    in_specs = [pl.BlockSpec((1, R * BAND, CG * GROUP), cur_map),
                pl.BlockSpec((1, R * BAND, kw), kcur_map),
                pl.BlockSpec((1, BAND, kw), kprev_map),
                pl.BlockSpec((1, R * BAND, kw), kcur_map),
                pl.BlockSpec((1, BAND, kw), kprev_map)]
    args = [q, k, k, v, v]
    out_shape = [jax.ShapeDtypeStruct((B, L, C), BF16)]
    out_specs = [pl.BlockSpec((1, R * BAND, CG * GROUP), cur_map)]
    if gqa:
        in_specs.append(_resident((1, C // HEAD_DIM)))
        args.append(sinks)
    else:
        out_shape.append(jax.ShapeDtypeStruct((B, C // (gpc * GROUP), L, LANES), F32))
        out_specs.append(pl.BlockSpec((1, CG // gpc, R * BAND, LANES), lambda b, ci, i: (b, ci, i, 0)))
    kern = functools.partial(_band_attn_kernel, R=R, CG=CG, gpc=gpc, gqa=gqa, max_dist_inclusive=not gqa,
                             want_lse=not gqa)
    return pl.pallas_call(
        kern, grid=(B, ncol, nrow), in_specs=in_specs, out_specs=out_specs, out_shape=out_shape,
        compiler_params=_cparams(("arbitrary", "arbitrary", "arbitrary")),
        name="band_attn_gqa" if gqa else f"band_attn_c{C}",
    )(*args)


def _layer_norm(z, g, b):
    mu = jnp.mean(z, axis=-1, keepdims=True)
    zc = z - mu
    var = jnp.mean(zc * zc, axis=-1, keepdims=True)
    return zc * lax.rsqrt(var + LN_EPS) * g + b


def _oproj_kernel(*refs, dils, tm, alpha, has_bias):
    nb = len(dils)
    o_refs = refs[:nb]
    pos = nb
    lse_refs = ()
    if nb > 1:
        lse_refs = refs[pos:pos + nb]
        pos += nb
        e_ref = refs[pos]
        pos += 1
    x_ref, w_ref = refs[pos], refs[pos + 1]
    pos += 2
    b_ref = None
    if has_bias:
        b_ref = refs[pos]
        pos += 1
    g_ref, beta_ref, out_ref = refs[pos], refs[pos + 1], refs[pos + 2]
    scr = refs[pos + 3:]

    if nb == 1:
        merged = o_refs[0][0].astype(BF16)
    else:
        os_, ls = [], []
        si = 0
        for n, dil in enumerate(dils):
            if dil == 1:
                os_.append(o_refs[n][0].astype(F32))
                ls.append(lse_refs[n][0, 0])
            else:
                so, sl = scr[si], scr[si + 1]
                si += 2
                nch = so.shape[0]
                for r in range(dil):
                    for jc in range(nch):
                        lo = (r * nch + jc) * LANES
                        so.at[jc][pl.ds(r, tm // dil, stride=dil), :] = o_refs[n][0, :, lo:lo + LANES].astype(F32)
                    sl[pl.ds(r, tm // dil, stride=dil), :] = lse_refs[n][0, r]
                os_.append(jnp.concatenate([so[jc] for jc in range(nch)], axis=1))
                ls.append(sl[...])
        m = functools.reduce(jnp.maximum, ls)
        ws = [jnp.exp(l - m) for l in ls]
        tot = functools.reduce(lambda a, b: a + b, ws)
        acc = None
        for o, w_ in zip(os_, ws):
            w_ = w_ / tot
            w_hi = w_.astype(BF16)
            w_lo = (w_ - w_hi.astype(F32)).astype(BF16)
            wexp = (jnp.dot(w_hi, e_ref[...], preferred_element_type=F32)
                    + jnp.dot(w_lo, e_ref[...], preferred_element_type=F32))
            acc = wexp * o if acc is None else acc + wexp * o
        merged = acc.astype(BF16)
    y = jnp.dot(merged, w_ref[...], preferred_element_type=F32)
    if b_ref is not None:
        y = y + b_ref[...]
    z = alpha * x_ref[0] + y
    out_ref[0] = _layer_norm(z, g_ref[...], beta_ref[...])


def _oproj_ln(os_, lses, x, w, bias, g, beta, *, dils, tm, alpha):
    B, S, D = x.shape
    C = w.shape[0]
    nb = len(dils)
    in_specs, args, scratch = [], [], []
    for o, dil in zip(os_, dils):
        in_specs.append(pl.BlockSpec((1, tm // dil, dil * C), lambda b, i: (b, i, 0)))
        args.append(o)
    if nb > 1:
        for l, dil in zip(lses, dils):
            in_specs.append(pl.BlockSpec((1, dil, tm // dil, LANES), lambda b, i: (b, 0, i, 0)))
            args.append(l)
            if dil > 1:
                scratch += [pltpu.VMEM((C // LANES, tm, LANES), F32), pltpu.VMEM((tm, LANES), F32)]
        expand = (jnp.arange(C)[None, :] // HEAD_DIM == jnp.arange(LANES)[:, None]).astype(BF16)
        in_specs.append(_resident((LANES, C)))
        args.append(expand)
    in_specs += [pl.BlockSpec((1, tm, D), lambda b, i: (b, i, 0)), _resident((C, D))]
    args += [x, w]
    if bias is not None:
        in_specs.append(_resident((1, D)))
        args.append(bias)
    in_specs += [_resident((1, D))] * 2
    args += [g, beta]
    kern = functools.partial(_oproj_kernel, dils=dils, tm=tm, alpha=alpha, has_bias=bias is not None)
    return pl.pallas_call(
        kern, grid=(B, S // tm), in_specs=in_specs,
        out_specs=pl.BlockSpec((1, tm, D), lambda b, i: (b, i, 0)),
        out_shape=jax.ShapeDtypeStruct((B, S, D), F32),
        scratch_shapes=scratch,
        compiler_params=_cparams(("arbitrary", "arbitrary")),
        name=f"oproj_ln_{nb}",
    )(*args)


FF_CHUNK = 256
FFN_TM = 1024


def _ff_chunks(ff, width):
    chunks = [(lo, width) for lo in range(0, ff - width + 1, width)]
    done = len(chunks) * width
    return chunks + ([(done, ff - done)] if done < ff else [])


def _ffn_kernel(*refs, tm, ff, alpha, seq_conv, fc):
    if seq_conv:
        x_ref, wup_ref, cw_ref, cb_ref, wdn_ref, g_ref, beta_ref, out_ref, st_ref, g_scr, carry_scr, act_scr = refs
    else:
        (x_ref, p0_ref, p1_ref, wup_ref, cw_ref, cb_ref, wdn_ref, g_ref, beta_ref,
         out_ref, st_ref, act_scr) = refs
    i = pl.program_id(1)
    last = pl.num_programs(1) - 1
    x = x_ref[0]
    xb = x.astype(BF16)
    for lo, w in _ff_chunks(ff, fc):
        cols = slice(lo, lo + w)
        gate = jnp.dot(xb, wup_ref[:, cols], preferred_element_type=F32)
        up = jnp.dot(xb, wup_ref[:, ff + lo:ff + lo + w], preferred_element_type=F32)
        if seq_conv:
            carry = carry_scr[:, cols]
            g_scr[0:8, :w] = jnp.where(i > 0, carry, jnp.zeros_like(carry))
            g_scr[8:8 + tm, :w] = gate
            g1 = g_scr[7:7 + tm, :w]
            g2 = g_scr[6:6 + tm, :w]
            carry_scr[:, cols] = gate[tm - 8:, :]

            @pl.when(i == last)
            def _():
                st_ref[0, :, cols] = gate[tm - (CONV_W - 1):, :]
        else:
            g2 = p0_ref[:, cols]
            g1 = p1_ref[:, cols]
            st_ref[0, :, cols] = gate
        cw = cw_ref[:, cols]
        gc = cb_ref[:, cols] + cw[0:1] * g2 + cw[1:2] * g1 + cw[2:3] * gate
        act = gc * jax.nn.sigmoid(gc) * up
        act_scr[:, cols] = act.astype(BF16)
    y = jnp.dot(act_scr[...], wdn_ref[...], preferred_element_type=F32)
    out_ref[0] = _layer_norm(alpha * x + y, g_ref[...], beta_ref[...])


def _ffn_ln(x, prev, w_up, cw, cb, w_dn, g, beta, *, tm, alpha, fc=FF_CHUNK):
    B, S, D = x.shape
    ff = w_dn.shape[0]
    seq_conv = prev is None
    in_specs = [pl.BlockSpec((1, tm, D), lambda b, i: (b, i, 0))]
    args = [x]
    if not seq_conv:
        in_specs += [pl.BlockSpec((tm, ff), lambda b, i: (i, 0))] * 2
        args += list(prev)
    in_specs += [_resident((D, 2 * ff)), _resident((CONV_W, ff)), _resident((1, ff)), _resident((ff, D)),
                 _resident((1, D)), _resident((1, D))]
    args += [w_up, cw, cb, w_dn, g, beta]
    if seq_conv:
        st_shape = jax.ShapeDtypeStruct((B, CONV_W - 1, ff), F32)
        st_spec = pl.BlockSpec((1, CONV_W - 1, ff), lambda b, i: (b, 0, 0))
        scratch = [pltpu.VMEM((tm + 8, min(fc, ff)), F32), pltpu.VMEM((8, ff), F32), pltpu.VMEM((tm, ff), BF16)]
    else:
        st_shape = jax.ShapeDtypeStruct((B, S, ff), F32)
        st_spec = pl.BlockSpec((1, tm, ff), lambda b, i: (b, i, 0))
        scratch = [pltpu.VMEM((tm, ff), BF16)]
    kern = functools.partial(_ffn_kernel, tm=tm, ff=ff, alpha=alpha, seq_conv=seq_conv, fc=fc)
    return pl.pallas_call(
        kern, grid=(B, S // tm), in_specs=in_specs,
        out_specs=[pl.BlockSpec((1, tm, D), lambda b, i: (b, i, 0)), st_spec],
        out_shape=[jax.ShapeDtypeStruct((B, S, D), F32), st_shape],
        scratch_shapes=scratch,
        compiler_params=_cparams(("arbitrary", "arbitrary")),
        name="ffn_ln_seq" if seq_conv else "ffn_ln_rows",
    )(*args)


def _proj_sample_kernel(*refs, has_bias):
    if has_bias:
        x_ref, w_ref, b_ref, c_ref, sa_ref, sb_ref, nat_ref, tr_ref = refs
    else:
        x_ref, w_ref, c_ref, sa_ref, sb_ref, nat_ref, tr_ref = refs
    y = jnp.dot(x_ref[...].astype(BF16), w_ref[...], preferred_element_type=F32)
    if has_bias:
        y = y + b_ref[...]
    outs = []
    for j in range(y.shape[1] // LANES):
        sl = slice(j * LANES, (j + 1) * LANES)
        yc = y[:, sl]
        outs.append(yc * c_ref[:, sl] + pltpu.roll(yc, LANES - ROT_HALF, 1) * sa_ref[:, sl]
                    + pltpu.roll(yc, ROT_HALF, 1) * sb_ref[:, sl])
    y = jnp.concatenate(outs, axis=1)
    nat_ref[0] = y
    tr_ref[0] = y.T


def _proj_sample(x, w, bias, tabs, gw):
    N, D = x.shape
    n = w.shape[1] // gw
    in_specs = [pl.BlockSpec((N, D), lambda j: (0, 0)), pl.BlockSpec((D, gw), lambda j: (0, j))]
    args = [x, w]
    if bias is not None:
        in_specs.append(pl.BlockSpec((1, gw), lambda j: (0, j)))
        args.append(bias)
    in_specs += [pl.BlockSpec((1, gw), lambda j: (0, j))] * 3
    args += list(tabs)
    return pl.pallas_call(
        functools.partial(_proj_sample_kernel, has_bias=bias is not None),
        grid=(n,), in_specs=in_specs,
        out_specs=[pl.BlockSpec((1, N, gw), lambda j: (j, 0, 0)), pl.BlockSpec((1, gw, N), lambda j: (j, 0, 0))],
        out_shape=[jax.ShapeDtypeStruct((n, N, gw), F32), jax.ShapeDtypeStruct((n, gw, N), F32)],
        compiler_params=_cparams(("arbitrary",)),
        name="proj_sample",
    )(*args)


def _pick_column(x, b):
    lane = lax.broadcasted_iota(jnp.int32, x.shape, 1)
    return jnp.sum(jnp.where(lane == b, x, 0.0), axis=1, keepdims=True)


def _lane_chunks(x):
    return [x[:, j * LANES:(j + 1) * LANES] for j in range(x.shape[1] // LANES)]


def _tree(op, xs):
    while len(xs) > 1:
        xs = [op(xs[i], xs[i + 1]) if i + 1 < len(xs) else xs[i] for i in range(0, len(xs), 2)]
    return xs[0]


def _roll_in(buf_t, new_col):
    L = buf_t.shape[1]
    lane = lax.broadcasted_iota(jnp.int32, buf_t.shape, 1)
    return jnp.where(lane == L - 1, new_col, pltpu.roll(buf_t, L - 1, 1))


def _sample_attn_a_kernel(qkv_ref, c1_ref, c2_ref, c3_ref, n1_ref, n2_ref, n3_ref, ot_ref, *, hpc):
    _sample_attn_a_block(pl.program_id(0), pl.program_id(1), qkv_ref, (c1_ref, c2_ref, c3_ref),
                         (n1_ref, n2_ref, n3_ref), ot_ref, hpc)


def _sample_attn_a_block(b, hc, qkv_ref, c_refs, n_refs, ot_ref, hpc):
    c1_ref, c2_ref, c3_ref = c_refs
    n1_ref, n2_ref, n3_ref = n_refs
    rows = hpc * HEAD_DIM
    cols = [_pick_column(qkv_ref[t], b) for t in range(3 * len(DILATIONS))]
    nb = ot_ref.shape[1]
    lane_b = lax.broadcasted_iota(jnp.int32, (HEAD_DIM, nb), 1)

    @pl.when(b == 0)
    def _():
        ot_ref[pl.ds(pl.multiple_of(hc * rows, rows), rows), :] = jnp.zeros((rows, nb), F32)

    for hh in range(hpc):
        hs = slice(hh * HEAD_DIM, (hh + 1) * HEAD_DIM)
        branches = list(zip(DILATIONS, (c1_ref, c2_ref, c3_ref), (n1_ref, n2_ref, n3_ref)))
        ss, s_news = [], []
        for g, (dil, c_ref, _) in enumerate(branches):
            qc, kc = cols[3 * g][hs], cols[3 * g + 1][hs]
            kt = c_ref[0, 0, hh]
            lane = lax.broadcasted_iota(jnp.int32, (1, kt.shape[1]), 1)
            s = jnp.sum(qc * kt, axis=0, keepdims=True)
            ss.append(jnp.where(lane % dil == 0, s, NEG_INF))
            s_news.append(jnp.sum(qc * kc, axis=0, keepdims=True))
        m = jnp.max(_tree(jnp.maximum, [c for s in ss for c in _lane_chunks(s)]), axis=1, keepdims=True)
        m = _tree(jnp.maximum, [m] + s_news)
        es = [jnp.exp(s - m) for s in ss]
        e_news = [jnp.exp(s - m) for s in s_news]
        den = jnp.sum(_tree(jnp.add, [c for e in es for c in _lane_chunks(e)]), axis=1, keepdims=True)
        den = _tree(jnp.add, [den] + e_news)
        pv, o_new = [], []
        for g, (dil, c_ref, n_ref) in enumerate(branches):
            kc, vc = cols[3 * g + 1][hs], cols[3 * g + 2][hs]
            vt = c_ref[0, 1, hh]
            pv += _lane_chunks(es[g] * vt)
            o_new.append(e_news[g] * vc)
            n_ref[0, 0, hh] = _roll_in(c_ref[0, 0, hh], kc)
            n_ref[0, 1, hh] = _roll_in(vt, vc)
        o = (jnp.sum(_tree(jnp.add, pv), axis=1, keepdims=True) + _tree(jnp.add, o_new)) * (1.0 / den)
        r0 = pl.multiple_of(hc * rows + hh * HEAD_DIM, HEAD_DIM)
        old = ot_ref[pl.ds(r0, HEAD_DIM), :]
        ot_ref[pl.ds(r0, HEAD_DIM), :] = jnp.where(lane_b == b, o, old)


def _sample_attn_a(qkv_t, caches, *, hpc):
    _, C, N = qkv_t.shape
    H = C // HEAD_DIM

    def cspec(c):
        L = c.shape[-1]
        return pl.BlockSpec((1, 2, hpc, HEAD_DIM, L), lambda b, hc: (b, 0, hc, 0, 0))

    in_specs = [pl.BlockSpec((qkv_t.shape[0], hpc * HEAD_DIM, N), lambda b, hc: (0, hc, 0))]
    in_specs += [cspec(c) for c in caches]
    out_specs = [cspec(c) for c in caches] + [pl.BlockSpec((C, N), lambda b, hc: (0, 0))]
    out_shape = [jax.ShapeDtypeStruct(c.shape, F32) for c in caches] + [jax.ShapeDtypeStruct((C, N), F32)]
    return pl.pallas_call(
        functools.partial(_sample_attn_a_kernel, hpc=hpc),
        grid=(N, H // hpc), in_specs=in_specs, out_specs=out_specs, out_shape=out_shape,
        compiler_params=_cparams(("arbitrary", "arbitrary")),
        name="sample_attn_a",
    )(qkv_t, *caches)


SAMPLE_B_PER_STEP = 8


def _sample_attn_b_kernel(q_ref, k_ref, v_ref, sink_ref, c_ref, n_ref, o_ref, *, n_heads, nbs):
    gqa = n_heads // N_KV_HEADS_B
    for bi in range(nbs):
        for kv in range(N_KV_HEADS_B):
            ds_ = slice(kv * HEAD_DIM, (kv + 1) * HEAD_DIM)
            hs = slice(kv * gqa, (kv + 1) * gqa)
            kt = _roll_in(c_ref[bi, 0, kv], k_ref[bi, ds_, :])
            vt = _roll_in(c_ref[bi, 1, kv], v_ref[bi, ds_, :])
            n_ref[bi, 0, kv] = kt
            n_ref[bi, 1, kv] = vt
            L = kt.shape[1]
            lane = lax.broadcasted_iota(jnp.int32, (1, L), 1)
            q = q_ref[bi, hs, :].astype(BF16)
            s = jnp.dot(q, kt.astype(BF16), preferred_element_type=F32)
            s = jnp.where(L - 1 - lane < WINDOW_B, s, NEG_INF)
            sink = sink_ref[hs, :]
            m = jnp.maximum(jnp.max(s, axis=1, keepdims=True), sink)
            e = jnp.exp(s - m)
            den = jnp.sum(e, axis=1, keepdims=True) + jnp.exp(sink - m)
            p = (e * (1.0 / den)).astype(BF16)
            o_ref[bi, hs, :] = lax.dot_general(p, vt.astype(BF16), (((1,), (1,)), ((), ())),
                                               preferred_element_type=F32)


def _sample_attn_b(q3, k_new, v_new, sinks, cache):
    N, H, _ = q3.shape
    nbs = math.gcd(SAMPLE_B_PER_STEP, N)

    def step(shape):
        return pl.BlockSpec((nbs,) + shape[1:], lambda b: (b,) + (0,) * (len(shape) - 1))

    return pl.pallas_call(
        functools.partial(_sample_attn_b_kernel, n_heads=H, nbs=nbs),
        grid=(N // nbs,),
        in_specs=[step(q3.shape), step(k_new.shape), step(v_new.shape), _resident(sinks.shape), step(cache.shape)],
        out_specs=[step(cache.shape), step(q3.shape)],
        out_shape=[jax.ShapeDtypeStruct(cache.shape, F32), jax.ShapeDtypeStruct(q3.shape, F32)],
        compiler_params=_cparams(("arbitrary",)),
        name="sample_attn_b",
    )(q3, k_new, v_new, sinks, cache)


def _to_positions_minor(c):
    return jnp.transpose(c, (0, 1, 3, 4, 2))


def _from_positions_minor(c):
    return jnp.transpose(c, (0, 1, 4, 2, 3))


def kernel(x_prompt, x_sample, cache_a1, cache_a2, cache_a3, cache_b, state_ffn, w_qkv_a, w_o_a, w_qkv_b,
           b_qkv_b, sinks_b, w_o_b, b_o_b, w_up, conv_w, conv_b, w_down, ln_g, ln_b):
    B, S, D = x_prompt.shape
    N, T, _ = x_sample.shape
    assert T == 1, "the sample group is decoded one token per request"
    depth = w_up.shape[0]
    ff = w_down.shape[1]
    alpha = (2.0 * depth) ** 0.25
    H = D // HEAD_DIM
    gpc = D // GROUP
    nq_b = w_o_b.shape[1]
    kvw_b = N_KV_HEADS_B * HEAD_DIM
    caches_a = (cache_a1, cache_a2, cache_a3)
    tm = min(512, S)

    pos_p = jnp.arange(S, dtype=jnp.int32)
    tabs_p = _rope_lane_tables(pos_p, LANES)
    pos_s = PAST_LEN + jnp.arange(T, dtype=jnp.int32)
    c_s, sa_s, sb_s = _rope_lane_tables(pos_s, D)

    xp = x_prompt
    xs = x_sample.reshape(1, N, D)
    new_a_p = [[] for _ in DILATIONS]
    new_a_s = [[] for _ in DILATIONS]
    new_b_p, new_b_s, new_f_p, new_f_s = [], [], [], []

    for i in range(depth):
        j = i // 2
        g1, b1 = ln_g[i, 0][None], ln_b[i, 0][None]
        g2, b2 = ln_g[i, 1][None], ln_b[i, 1][None]
        if i % 2 == 0:
            w_all = w_qkv_a[j].astype(BF16)
            wo = w_o_a[j].astype(BF16)
            os_, lses = [], []
            for g, dil in enumerate(DILATIONS):
                q, k, v, st = _proj_prompt(
                    xp, w_all[:, g * 3 * D:(g + 1) * 3 * D], None, tabs_p,
                    wq=D, wkv=D, dil=dil, keep=min(BAND * dil, S), tm=tm)
                R = min(UNITS_PER_STEP // gpc, (S // dil) // BAND)
                cg = gpc * max(1, min(dil, (UNITS_PER_STEP // R) // gpc))
                o, lse = _band_attn(q, k, v, None, R=R, CG=cg, gpc=gpc, gqa=False)
                os_.append(o)
                lses.append(lse)
                new_a_p[g].append(st.reshape(B, 2, st.shape[2], H, HEAD_DIM))
            hp = _oproj_ln(os_, lses, xp, wo, None, g1, b1, dils=DILATIONS, tm=tm, alpha=alpha)
            one = jnp.ones((1, D), F32)
            zero = jnp.zeros((1, D), F32)
            c_tab = jnp.concatenate([c_s * SCALE, c_s, one] * len(DILATIONS), axis=1)
            sa_tab = jnp.concatenate([sa_s * SCALE, sa_s, zero] * len(DILATIONS), axis=1)
            sb_tab = jnp.concatenate([sb_s * SCALE, sb_s, zero] * len(DILATIONS), axis=1)
            nat, tr = _proj_sample(xs[0], w_all, None, (c_tab, sa_tab, sb_tab), D)
            outs = _sample_attn_a(tr, [_to_positions_minor(c[j]) for c in caches_a],
                                  hpc=min(SAMPLE_A_HEADS_PER_STEP, H))
            for g in range(len(DILATIONS)):
                new_a_s[g].append(_from_positions_minor(outs[g]))
            hs = _oproj_ln([outs[-1].T[None]], None, xs, wo, None, g1, b1, dils=(1,), tm=N, alpha=alpha)
        else:
            w_all = w_qkv_b[j].astype(BF16)
            wo = w_o_b[j].astype(BF16)
            bias = b_qkv_b[j][None]
            sinks = sinks_b[j][None]
            q, k, v, st = _proj_prompt(xp, w_all, bias, tabs_p, wq=nq_b, wkv=kvw_b, dil=1,
                                       keep=min(WINDOW_B, S), tm=tm, kv_rep=True)
            (o,) = _band_attn(q, k, v, sinks, R=min(UNITS_PER_STEP // (nq_b // GROUP), S // BAND),
                              CG=nq_b // GROUP, gpc=nq_b // GROUP, gqa=True)
            new_b_p.append(st.reshape(B, 2, st.shape[2], N_KV_HEADS_B, HEAD_DIM))
            hp = _oproj_ln([o], None, xp, wo, b_o_b[j][None], g1, b1, dils=(1,), tm=tm, alpha=alpha)
            nw = w_all.shape[1]
            reps = nw // D + 1
            c_full = jnp.concatenate([c_s * SCALE] + [c_s] * reps, axis=1)
            sa_full = jnp.concatenate([sa_s * SCALE] + [sa_s] * reps, axis=1)
            sb_full = jnp.concatenate([sb_s * SCALE] + [sb_s] * reps, axis=1)
            is_v = (jnp.arange(nw) >= nq_b + kvw_b)[None]
            c_tab = jnp.where(is_v, 1.0, c_full[:, :nw])
            sa_tab = jnp.where(is_v, 0.0, sa_full[:, :nw])
            sb_tab = jnp.where(is_v, 0.0, sb_full[:, :nw])
            nat, tr = _proj_sample(xs[0], w_all, bias, (c_tab, sa_tab, sb_tab), nw)
            nat = nat[0]
            new_c, o3 = _sample_attn_b(nat[:, :nq_b].reshape(N, nq_b // HEAD_DIM, HEAD_DIM),
                                       nat[:, nq_b:nq_b + kvw_b].reshape(N, kvw_b, 1),
                                       nat[:, nq_b + kvw_b:].reshape(N, kvw_b, 1),
                                       sinks_b[j][:, None], _to_positions_minor(cache_b[j]))
            new_b_s.append(_from_positions_minor(new_c))
            hs = _oproj_ln([o3.reshape(1, N, nq_b)], None, xs, wo, b_o_b[j][None], g1, b1, dils=(1,), tm=N,
                           alpha=alpha)
        wup = w_up[i].astype(BF16)
        wdn = w_down[i].astype(BF16)
        cb = conv_b[i][None]
        xp, st_p = _ffn_ln(hp, None, wup, conv_w[i], cb, wdn, g2, b2, tm=min(FFN_TM, S), alpha=alpha)
        xs, g_s = _ffn_ln(hs, (state_ffn[i][:, 0], state_ffn[i][:, 1]), wup, conv_w[i], cb, wdn, g2, b2,
                          tm=N, alpha=alpha)
        new_f_p.append(st_p)
        new_f_s.append(jnp.stack([state_ffn[i][:, 1], g_s[0]], axis=1))

    return (xp, xs.reshape(N, T, D),
            jnp.stack(new_a_p[0]), jnp.stack(new_a_s[0]),
            jnp.stack(new_a_p[1]), jnp.stack(new_a_s[1]),
            jnp.stack(new_a_p[2]), jnp.stack(new_a_s[2]),
            jnp.stack(new_b_p), jnp.stack(new_b_s),
            jnp.stack(new_f_p), jnp.stack(new_f_s))
```
